```python
import jax, jax.numpy as jnp
from jax import lax
import numpy as np

D_MODEL = 1024
BATCH = 8
SEQ = 4096
DEPTH = 4

CHUNK = 64
MEM_TOKENS = 256
N_EVEN = (DEPTH + 1) // 2
N_ODD = DEPTH // 2
HGRN_HEADS = 4
HGRN_DK = 128
HGRN_DV = 128
HGRN_KEY = HGRN_HEADS * HGRN_DK
HGRN_VAL = HGRN_HEADS * HGRN_DV
SSD_HEADS = 8
SSD_HEADDIM = 64
SSD_DINNER = SSD_HEADS * SSD_HEADDIM
SSD_GROUPS = 2
SSD_STATE = 128
SSD_CONV = 4
SSD_CONV_CH = SSD_DINNER + 2 * SSD_GROUPS * SSD_STATE
AB_SPLIT_SIZES = (HGRN_KEY, HGRN_KEY, HGRN_VAL, HGRN_VAL, SSD_DINNER, SSD_CONV_CH, SSD_HEADS)
AB_IN = sum(AB_SPLIT_SIZES)
AB_OUT = HGRN_VAL + SSD_DINNER
CONF_KERNEL = 31
XATTN_HEADS = 4
XATTN_HD = D_MODEL // XATTN_HEADS
D_FF = 4 * D_MODEL
EPS = 1e-6

kernel_name = "hybrid_hgrn2_ssd_conformer_trunk"


def rms_normalize(x):
    xf = x.astype(jnp.float32)
    return xf * lax.rsqrt(jnp.mean(xf * xf, axis=-1, keepdims=True) + EPS)


def rms_norm(x, w):
    return (rms_normalize(x) * w.astype(jnp.float32)).astype(x.dtype)


def layer_norm(x, w, b):
    xf = x.astype(jnp.float32)
    mu = jnp.mean(xf, axis=-1, keepdims=True)
    var = jnp.mean(jnp.square(xf - mu), axis=-1, keepdims=True)
    y = (xf - mu) * lax.rsqrt(var + EPS) * w.astype(jnp.float32) + b.astype(jnp.float32)
    return y.astype(x.dtype)


def causal_depthwise_conv(x, w, b):
    K, C = w.shape
    xp = jnp.pad(x, ((0, 0), (K - 1, 0), (0, 0)))
    y = lax.conv_general_dilated(xp, w[:, None, :].astype(x.dtype), window_strides=(1,), padding='VALID',
                                 dimension_numbers=('NWC', 'WIO', 'NWC'), feature_group_count=C)
    return y + b


def masked_exp(diff, mask):
    return jnp.where(mask, jnp.exp(jnp.where(mask, diff, 0.0)), 0.0)


def segsum_exp(a):
    L = a.shape[-1]
    cs = jnp.cumsum(a, axis=-1)
    mask = jnp.tril(jnp.ones((L, L), dtype=bool))
    return masked_exp(cs[..., :, None] - cs[..., None, :], mask)


def hgrn2_chunk_scan(q, k, v, log_f):
    Bsz, T, H, DK = q.shape
    DV = v.shape[-1]
    n_chunks = T // CHUNK

    def to_chunks(a):
        return a.reshape(Bsz, n_chunks, CHUNK, H, a.shape[-1]).transpose(1, 0, 3, 2, 4)

    mask = jnp.tril(jnp.ones((CHUNK, CHUNK), dtype=bool))[:, :, None]

    def step(S, inp):
        qi, ki, vi, gi = inp
        b = jnp.cumsum(gi, axis=2)
        rel = b[:, :, :, None, :] - b[:, :, None, :, :]
        decay = masked_exp(rel, mask)
        scores = jnp.einsum('bhtd,bhsd,bhtsd->bhts', qi, ki, decay)
        o = jnp.einsum('bhts,bhsv->bhtv', scores, vi) + jnp.einsum('bhtd,bhdv->bhtv', qi * jnp.exp(b), S)
        b_last = b[:, :, -1:, :]
        S_new = S * jnp.exp(b_last[:, :, 0, :, None]) + jnp.einsum('bhsd,bhsv->bhdv', ki * jnp.exp(b_last - b), vi)
        return S_new, o

    S0 = jnp.zeros((Bsz, H, DK, DV), jnp.float32)
    _, o = lax.scan(step, S0, (to_chunks(q), to_chunks(k), to_chunks(v), to_chunks(log_f)))
    return o.transpose(1, 0, 3, 2, 4).reshape(Bsz, T, H, DV)


def ssd_chunked_scan(xs, dt, A, Bm, Cm):
    Bsz, T, H, P = xs.shape
    G, N = Bm.shape[2], Bm.shape[3]
    J = H // G
    C = T // CHUNK
    L = CHUNK
    xc = (xs * dt[..., None]).reshape(Bsz, C, L, G, J, P)
    dA = (dt * A).reshape(Bsz, C, L, G, J).transpose(0, 3, 4, 1, 2)
    Bc = Bm.reshape(Bsz, C, L, G, N)
    Cc = Cm.reshape(Bsz, C, L, G, N)
    A_cs = jnp.cumsum(dA, axis=-1)
    decay_intra = segsum_exp(dA)
    cb = jnp.einsum('bclgn,bcsgn->bgcls', Cc, Bc)
    y_diag = jnp.einsum('bgjcls,bcsgjp->bclgjp', cb[:, :, None] * decay_intra, xc)
    decay_to_end = jnp.exp(A_cs[..., -1:] - A_cs)
    chunk_states = jnp.einsum('bclgn,bgjcl,bclgjp->bcgjpn', Bc, decay_to_end, xc)
    chunk_states = jnp.concatenate([jnp.zeros_like(chunk_states[:, :1]), chunk_states], axis=1)
    chunk_decay = jnp.pad(A_cs[..., -1], ((0, 0), (0, 0), (0, 0), (1, 0)))
    decay_chunk = segsum_exp(chunk_decay)
    states_in = jnp.einsum('bgjzc,bcgjpn->bzgjpn', decay_chunk, chunk_states)[:, :-1]
    y_off = jnp.einsum('bclgn,bcgjpn,bgjcl->bclgjp', Cc, states_in, jnp.exp(A_cs))
    return (y_diag + y_off).reshape(Bsz, T, H, P)


def hgrn_ssd_mixer(h, w_in, lb, out_norm_w, conv_w, conv_b, dt_bias, a_log, d_skip, ssd_norm_w, w_out):
    Bsz, T, _ = h.shape
    points = [int(p) for p in np.cumsum(AB_SPLIT_SIZES)[:-1]]
    q, f_raw, i_val, g, z, xbc, dt_raw = jnp.split(h @ w_in, points, axis=-1)
    f32 = f_raw.astype(jnp.float32)
    log_f = jnp.log(lb + (1.0 - lb) * jax.nn.sigmoid(f32))
    k = (1.0 - lb) * jax.nn.sigmoid(-f32)
    qh = q.astype(jnp.float32).reshape(Bsz, T, HGRN_HEADS, HGRN_DK) * (HGRN_DK ** -0.5)
    o_a = hgrn2_chunk_scan(qh, k.reshape(Bsz, T, HGRN_HEADS, HGRN_DK),
                           i_val.astype(jnp.float32).reshape(Bsz, T, HGRN_HEADS, HGRN_DV),
                           log_f.reshape(Bsz, T, HGRN_HEADS, HGRN_DK))
    o_a = rms_normalize(o_a).reshape(Bsz, T, HGRN_VAL) * out_norm_w.astype(jnp.float32)
    o_a = (o_a * jax.nn.silu(g.astype(jnp.float32))).astype(h.dtype)
    xbc = jax.nn.silu(causal_depthwise_conv(xbc, conv_w, conv_b)).astype(jnp.float32)
    xs, Bm, Cm = jnp.split(xbc, [SSD_DINNER, SSD_DINNER + SSD_GROUPS * SSD_STATE], axis=-1)
    xs = xs.reshape(Bsz, T, SSD_HEADS, SSD_HEADDIM)
    dt = jax.nn.softplus(dt_raw.astype(jnp.float32) + dt_bias.astype(jnp.float32))
    A = -jnp.exp(a_log.astype(jnp.float32))
    y = ssd_chunked_scan(xs, dt, A, Bm.reshape(Bsz, T, SSD_GROUPS, SSD_STATE),
                         Cm.reshape(Bsz, T, SSD_GROUPS, SSD_STATE))
    y = (y + xs * d_skip.astype(jnp.float32)[:, None]).reshape(Bsz, T, SSD_DINNER)
    y = y * jax.nn.silu(z.astype(jnp.float32))
    y = rms_normalize(y.reshape(Bsz, T, SSD_GROUPS, SSD_DINNER // SSD_GROUPS)).reshape(Bsz, T, SSD_DINNER)
    y = (y * ssd_norm_w.astype(jnp.float32)).astype(h.dtype)
    return jnp.concatenate([o_a, y], axis=-1) @ w_out


def conformer_conv_module(h, w_pw1, b_pw1, w_dw, b_dw, ln_w, ln_b, w_pw2, b_pw2):
    a, gate = jnp.split(h @ w_pw1 + b_pw1, 2, axis=-1)
    u = a * jax.nn.sigmoid(gate)
    u = causal_depthwise_conv(u, w_dw, b_dw)
    u = jax.nn.silu(layer_norm(u, ln_w, ln_b))
    return u @ w_pw2 + b_pw2


def memory_cross_attention(h, mem_n, wq, wk, wv, wo):
    Bsz, T, D = h.shape
    M = mem_n.shape[1]
    q = (h @ wq).reshape(Bsz, T, XATTN_HEADS, XATTN_HD)
    k = (mem_n @ wk).reshape(Bsz, M, XATTN_HEADS, XATTN_HD)
    v = (mem_n @ wv).reshape(Bsz, M, XATTN_HEADS, XATTN_HD)
    s = jnp.einsum('bthd,bmhd->bhtm', q, k).astype(jnp.float32) * (XATTN_HD ** -0.5)
    p = jax.nn.softmax(s, axis=-1).astype(v.dtype)
    o = jnp.einsum('bhtm,bmhd->bthd', p, v).reshape(Bsz, T, D)
    return o @ wo


def sq_relu_mlp(h, w1, w2):
    return jnp.square(jax.nn.relu(h @ w1)) @ w2


def setup_inputs(seed: int = 0) -> dict:
    key = jax.random.key(seed)
    ks = iter(jax.random.split(key, 64))
    f32 = jnp.float32

    def nrm(shape, scale):
        return jax.random.normal(next(ks), shape, f32) * scale

    def gain(shape):
        return 1.0 + 0.02 * jax.random.normal(next(ks), shape, f32)

    D = D_MODEL
    dt0 = jnp.exp(jax.random.uniform(next(ks), (N_EVEN, SSD_HEADS), f32) * (jnp.log(0.1) - jnp.log(0.001)) + jnp.log(0.001))
    return {
        "x": nrm((BATCH, SEQ, D), 1.0),
        "mem": nrm((BATCH, MEM_TOKENS, D), 1.0),
        "mem_norm_w": gain((D,)),
        "norm_mix_w": gain((DEPTH, D)),
        "ab_w_in": nrm((N_EVEN, D, AB_IN), D ** -0.5),
        "hgrn_lb_logits": nrm((N_EVEN, HGRN_KEY), 0.5),
        "hgrn_out_norm_w": gain((N_EVEN, HGRN_VAL)),
        "ssd_conv_w": nrm((N_EVEN, SSD_CONV, SSD_CONV_CH), SSD_CONV ** -0.5),
        "ssd_conv_b": nrm((N_EVEN, SSD_CONV_CH), 0.02),
        "ssd_dt_bias": dt0 + jnp.log(-jnp.expm1(-dt0)),
        "ssd_a_log": jnp.log(jax.random.uniform(next(ks), (N_EVEN, SSD_HEADS), f32, 1.0, 16.0)),
        "ssd_d": gain((N_EVEN, SSD_HEADS)),
        "ssd_norm_w": gain((N_EVEN, SSD_DINNER)),
        "ab_w_out": nrm((N_EVEN, AB_OUT, D), AB_OUT ** -0.5),
        "cv_w_pw1": nrm((N_ODD, D, 2 * D), D ** -0.5),
        "cv_b_pw1": nrm((N_ODD, 2 * D), 0.02),
        "cv_w_dw": nrm((N_ODD, CONF_KERNEL, D), CONF_KERNEL ** -0.5),
        "cv_b_dw": nrm((N_ODD, D), 0.02),
        "cv_ln_w": gain((N_ODD, D)),
        "cv_ln_b": nrm((N_ODD, D), 0.02),
        "cv_w_pw2": nrm((N_ODD, D, D), D ** -0.5),
        "cv_b_pw2": nrm((N_ODD, D), 0.02),
        "norm_xattn_w": gain((DEPTH, D)),
        "xattn_wq": nrm((DEPTH, D, D), D ** -0.5),
        "xattn_wk": nrm((DEPTH, D, D), D ** -0.5),
        "xattn_wv": nrm((DEPTH, D, D), D ** -0.5),
        "xattn_wo": nrm((DEPTH, D, D), D ** -0.5),
        "norm_mlp_w": gain((DEPTH, D)),
        "mlp_w1": nrm((DEPTH, D, D_FF), D ** -0.5),
        "mlp_w2": nrm((DEPTH, D_FF, D), D_FF ** -0.5),
        "final_norm_w": gain((D,)),
    }


def reference(x, mem, mem_norm_w, norm_mix_w, ab_w_in, hgrn_lb_logits, hgrn_out_norm_w,
              ssd_conv_w, ssd_conv_b, ssd_dt_bias, ssd_a_log, ssd_d, ssd_norm_w, ab_w_out,
              cv_w_pw1, cv_b_pw1, cv_w_dw, cv_b_dw, cv_ln_w, cv_ln_b, cv_w_pw2, cv_b_pw2,
              norm_xattn_w, xattn_wq, xattn_wk, xattn_wv, xattn_wo,
              norm_mlp_w, mlp_w1, mlp_w2, final_norm_w):
    mem_n = rms_norm(mem, mem_norm_w)
    p = jax.nn.softmax(hgrn_lb_logits.astype(jnp.float32), axis=0)
    lower_bounds = jnp.cumsum(p, axis=0) - p[0:1]
    for layer in range(DEPTH):
        h = rms_norm(x, norm_mix_w[layer])
        if layer % 2 == 0:
            e = layer // 2
            h = hgrn_ssd_mixer(h, ab_w_in[e], lower_bounds[e], hgrn_out_norm_w[e], ssd_conv_w[e],
                               ssd_conv_b[e], ssd_dt_bias[e], ssd_a_log[e], ssd_d[e], ssd_norm_w[e],
                               ab_w_out[e])
        else:
            o = layer // 2
            h = conformer_conv_module(h, cv_w_pw1[o], cv_b_pw1[o], cv_w_dw[o], cv_b_dw[o],
                                      cv_ln_w[o], cv_ln_b[o], cv_w_pw2[o], cv_b_pw2[o])
        x = x + h
        x = x + memory_cross_attention(rms_norm(x, norm_xattn_w[layer]), mem_n, xattn_wq[layer],
                                       xattn_wk[layer], xattn_wv[layer], xattn_wo[layer])
        x = x + sq_relu_mlp(rms_norm(x, norm_mlp_w[layer]), mlp_w1[layer], mlp_w2[layer])
    return rms_norm(x, final_norm_w)
```

```python
import functools

import jax
import jax.numpy as jnp
from jax import lax
from jax.experimental import pallas as pl
from jax.experimental.pallas import tpu as pltpu

F32 = jnp.float32
BF16 = jnp.bfloat16

EPS = 1e-6
CHUNK = 64
SUB = 16
LANES = 128
PAD_ROWS = 16

HGRN_HEADS = 4
HGRN_DK = 128
SSD_HEADS = 8
SSD_HEADDIM = 64
SSD_GROUPS = 2
SSD_STATE = 128
XATTN_HEADS = 4

VMEM_LIMIT_BYTES = 56 * 1024 * 1024


def _rms(x):
    return x * lax.rsqrt(jnp.mean(x * x, axis=-1, keepdims=True) + EPS)


def _sigmoid(x):
    return 1.0 / (1.0 + jnp.exp(-x))


def _silu(x):
    return x * _sigmoid(x)


def _dot(a, b):
    return jnp.dot(a, b, preferred_element_type=F32)


def _dot_nt(a, b):
    return lax.dot_general(a, b, (((1,), (1,)), ((), ())), preferred_element_type=F32)


def _dot_tn(a, b):
    return lax.dot_general(a, b, (((0,), (0,)), ((), ())), preferred_element_type=F32)


def _split3(a):
    a0 = a.astype(BF16)
    r1 = a - a0.astype(F32)
    a1 = r1.astype(BF16)
    a2 = (r1 - a1.astype(F32)).astype(BF16)
    return a0, a1, a2


def _dot01(m01, a):
    a0, a1, a2 = _split3(a)
    return _dot(m01, a0) + _dot(m01, a1) + _dot(m01, a2)


def _dot01_r(a, m01):
    a0, a1, a2 = _split3(a)
    return _dot(a0, m01) + _dot(a1, m01) + _dot(a2, m01)


def _resident(shape):
    nd = len(shape)
    return pl.BlockSpec(shape, lambda *_: (0,) * nd, pipeline_mode=pl.Buffered(1))


def _params(n_axes):
    return pltpu.CompilerParams(dimension_semantics=("arbitrary",) * n_axes,
                                vmem_limit_bytes=VMEM_LIMIT_BYTES)


def _kv_body(mem_ref, mw_ref, wk_ref, wv_ref, k_ref, v_ref):
    m = mem_ref[0]
    mn = (_rms(m) * mw_ref[...]).astype(BF16)
    k_ref[0, 0] = _dot(mn, wk_ref[0].astype(BF16)).astype(BF16)
    v_ref[0, 0] = _dot(mn, wv_ref[0].astype(BF16)).astype(BF16)


def _memory_kv(mem, mem_norm_w, wk, wv):
    depth, d, _ = wk.shape
    b, m, _ = mem.shape
    out = jax.ShapeDtypeStruct((depth, b, m, d), BF16)
    return pl.pallas_call(
        _kv_body,
        grid=(depth, b),
        in_specs=[
            pl.BlockSpec((1, m, d), lambda l, i: (i, 0, 0)),
            pl.BlockSpec((1, d), lambda l, i: (0, 0)),
            pl.BlockSpec((1, d, d), lambda l, i: (l, 0, 0)),
            pl.BlockSpec((1, d, d), lambda l, i: (l, 0, 0)),
        ],
        out_specs=[
            pl.BlockSpec((1, 1, m, d), lambda l, i: (l, i, 0, 0)),
            pl.BlockSpec((1, 1, m, d), lambda l, i: (l, i, 0, 0)),
        ],
        out_shape=[out, out],
        compiler_params=_params(2),
        name="memory_kv",
    )(mem, mem_norm_w.reshape(1, d), wk, wv)


def _mlp_body(x_ref, nw_ref, w1_ref, w2_ref, fw_ref, o_ref, *, ff_chunk, final):
    x = x_ref[...]
    hb = (_rms(x) * nw_ref[...]).astype(BF16)
    acc = x
    for c in range(w1_ref.shape[1] // ff_chunk):
        a = _dot(hb, w1_ref[:, c * ff_chunk:(c + 1) * ff_chunk])
        a = jnp.square(jnp.maximum(a, 0.0)).astype(BF16)
        acc = acc + _dot(a, w2_ref[c * ff_chunk:(c + 1) * ff_chunk, :])
    if final:
        acc = _rms(acc) * fw_ref[...]
    o_ref[...] = acc


def _mlp(x2, norm_w, w1, w2, final_w, *, final, tb=512, ff_chunk=1024):
    n, d = x2.shape
    dff = w1.shape[1]
    return pl.pallas_call(
        functools.partial(_mlp_body, ff_chunk=ff_chunk, final=final),
        grid=(n // tb,),
        in_specs=[
            pl.BlockSpec((tb, d), lambda i: (i, 0)),
            _resident((1, d)),
            _resident((d, dff)),
            _resident((dff, d)),
            _resident((1, d)),
        ],
        out_specs=pl.BlockSpec((tb, d), lambda i: (i, 0)),
        out_shape=jax.ShapeDtypeStruct((n, d), F32),
        compiler_params=_params(1),
        name="sq_relu_mlp",
    )(x2, norm_w.reshape(1, d), w1.astype(BF16), w2.astype(BF16), final_w.reshape(1, d))


def _attn_body(x_ref, nw_ref, wq_ref, k_ref, v_ref, wo_ref, o_ref, *, heads):
    x = x_ref[0]
    d = x.shape[-1]
    hd = d // heads
    hb = (_rms(x) * nw_ref[...]).astype(BF16)
    q = (_dot(hb, wq_ref[...]) * (hd ** -0.5)).astype(BF16)
    outs = []
    for h in range(heads):
        sl = slice(h * hd, (h + 1) * hd)
        s = _dot_nt(q[:, sl], k_ref[0, 0, :, sl])
        p = jnp.exp(s - jnp.max(s, axis=-1, keepdims=True))
        den = jnp.sum(p, axis=-1, keepdims=True)
        oh = _dot(p.astype(BF16), v_ref[0, 0, :, sl]) * (1.0 / den)
        outs.append(oh.astype(BF16))
    o = jnp.concatenate(outs, axis=-1)
    o_ref[0] = x + _dot(o, wo_ref[...])


def _cross_attention(x, norm_w, wq, k_all, v_all, wo, layer, *, tb=512):
    b, t, d = x.shape
    m = k_all.shape[2]
    return pl.pallas_call(
        functools.partial(_attn_body, heads=XATTN_HEADS),
        grid=(b, t // tb),
        in_specs=[
            pl.BlockSpec((1, tb, d), lambda i, j: (i, j, 0)),
            _resident((1, d)),
            _resident((d, d)),
            pl.BlockSpec((1, 1, m, d), lambda i, j: (layer, i, 0, 0)),
            pl.BlockSpec((1, 1, m, d), lambda i, j: (layer, i, 0, 0)),
            _resident((d, d)),
        ],
        out_specs=pl.BlockSpec((1, tb, d), lambda i, j: (i, j, 0)),
        out_shape=jax.ShapeDtypeStruct((b, t, d), F32),
        compiler_params=_params(2),
        name="memory_xattn",
    )(x, norm_w.reshape(1, d), wq.astype(BF16), k_all, v_all, wo.astype(BF16))


def _conf_body(x_ref, nw_ref, w1_ref, b1_ref, wdw_ref, bdw_ref, lnw_ref, lnb_ref, w2_ref, b2_ref,
               o_ref, u_ref, c_ref, *, taps, row_tile):
    tb, d = x_ref.shape[1], x_ref.shape[2]
    hist = u_ref.shape[0] - tb

    @pl.when(pl.program_id(1) == 0)
    def _():
        u_ref[0:hist, :] = jnp.zeros((hist, d), F32)

    x = x_ref[0]
    hb = (_rms(x) * nw_ref[...]).astype(BF16)
    ag = _dot(hb, w1_ref[...]) + b1_ref[...]
    u_ref[hist:hist + tb, :] = ag[:, :d] * _sigmoid(ag[:, d:])

    for r0 in range(0, tb, row_tile):
        acc = jnp.broadcast_to(bdw_ref[...], (row_tile, d))
        for j in range(taps):
            lo = r0 + hist - (taps - 1) + j
            acc = acc + wdw_ref[j:j + 1, :] * u_ref[lo:lo + row_tile, :]
        c_ref[r0:r0 + row_tile, :] = acc
    u_ref[0:hist, :] = u_ref[tb:tb + hist, :]

    c = c_ref[...]
    mu = jnp.mean(c, axis=-1, keepdims=True)
    cc = c - mu
    var = jnp.mean(cc * cc, axis=-1, keepdims=True)
    yn = cc * lax.rsqrt(var + EPS) * lnw_ref[...] + lnb_ref[...]
    ys = _silu(yn).astype(BF16)
    o_ref[0] = x + _dot(ys, w2_ref[...]) + b2_ref[...]


def _conformer(x, norm_w, w_pw1, b_pw1, w_dw, b_dw, ln_w, ln_b, w_pw2, b_pw2, *, tb=512, row_tile=32):
    b, t, d = x.shape
    taps = w_dw.shape[0]
    hist = 32
    assert taps - 1 <= hist
    return pl.pallas_call(
        functools.partial(_conf_body, taps=taps, row_tile=row_tile),
        grid=(b, t // tb),
        in_specs=[
            pl.BlockSpec((1, tb, d), lambda i, j: (i, j, 0)),
            _resident((1, d)),
            _resident((d, 2 * d)),
            _resident((1, 2 * d)),
            _resident((taps, d)),
            _resident((1, d)),
            _resident((1, d)),
            _resident((1, d)),
            _resident((d, d)),
            _resident((1, d)),
        ],
        out_specs=pl.BlockSpec((1, tb, d), lambda i, j: (i, j, 0)),
        out_shape=jax.ShapeDtypeStruct((b, t, d), F32),
        scratch_shapes=[pltpu.VMEM((hist + tb, d), F32), pltpu.VMEM((tb, d), F32)],
        compiler_params=_params(2),
        name="conformer_conv",
    )(x, norm_w.reshape(1, d), w_pw1.astype(BF16), b_pw1.reshape(1, 2 * d), w_dw, b_dw.reshape(1, d),
      ln_w.reshape(1, d), ln_b.reshape(1, d), w_pw2.astype(BF16), b_pw2.reshape(1, d))


def _iota2(shape, axis):
    return lax.broadcasted_iota(jnp.int32, shape, axis)


def _mix_body(x_ref, nw_ref, win_ref, lbl_ref, onw_ref, cw_ref, cb_ref, dtb_ref, alog_ref, dsk_ref,
              snw_ref, wout_ref, o_ref, pr_ref, kb_ref, bb_ref, sth_ref, sts_ref, y_ref, *, e):
    tb, d = x_ref.shape[1], x_ref.shape[2]
    hk = HGRN_HEADS * HGRN_DK
    sd = SSD_HEADS * SSD_HEADDIM
    gn = SSD_GROUPS * SSD_STATE
    c_q, c_f, c_i, c_g = 0, hk, 2 * hk, 3 * hk
    c_z = 4 * hk
    c_x = c_z + sd
    c_b = c_x + sd
    c_c = c_b + gn
    c_dt = c_c + gn
    hp = SSD_HEADS // SSD_GROUPS * SSD_HEADDIM
    taps = cw_ref.shape[0]
    P = PAD_ROWS

    @pl.when(pl.program_id(1) == 0)
    def _():
        pr_ref[0:P, :] = jnp.zeros((P, pr_ref.shape[1]), F32)
        kb_ref[0:P, :] = jnp.zeros((P, hk), F32)
        bb_ref[0:P, :] = jnp.zeros((P, hk), F32)
        sth_ref[...] = jnp.zeros(sth_ref.shape, F32)
        sts_ref[...] = jnp.zeros(sts_ref.shape, F32)

    x = x_ref[0]
    hb = (_rms(x) * nw_ref[...]).astype(BF16)
    pr_ref[P:P + tb, :] = _dot(hb, win_ref[...])

    lg = lbl_ref[...]
    ex = jnp.exp(lg - jnp.max(lg, axis=0, keepdims=True))
    probs = ex / jnp.sum(ex, axis=0, keepdims=True)
    lb = jnp.sum(probs[0:e + 1], axis=0, keepdims=True) - probs[0:1]

    r64 = _iota2((CHUNK, CHUNK), 0)
    c64 = _iota2((CHUNK, CHUNK), 1)
    same_sub = (r64 // SUB) == (c64 // SUB)
    tri_sub = jnp.where(same_sub & (c64 <= r64), 1.0, 0.0).astype(BF16)
    ones_sub = jnp.where(same_sub, 1.0, 0.0).astype(BF16)
    tri_chunk = jnp.where(c64 <= r64, 1.0, 0.0).astype(BF16)
    tri_chunk_t = jnp.where(r64 <= c64, 1.0, 0.0).astype(BF16)
    causal = c64 <= r64
    expand = jnp.where(_iota2((LANES, sd), 1) // SSD_HEADDIM == _iota2((LANES, sd), 0), 1.0, 0.0).astype(BF16)
    ones_k = jnp.ones((HGRN_DK, HGRN_DK), BF16)
    row_in_sub = _iota2((CHUNK, HGRN_DK), 0) % SUB

    a_neg = -jnp.exp(alog_ref[...])

    for c in range(tb // CHUNK):
        r0 = P + c * CHUNK
        rows = slice(r0, r0 + CHUNK)

        fr = pr_ref[rows, c_f:c_f + hk]
        lf = jnp.log(lb + (1.0 - lb) * _sigmoid(fr))
        kk = (1.0 - lb) * _sigmoid(-fr)
        bcs = _dot01(tri_sub, lf)
        btot = _dot01(ones_sub, lf)
        kb_ref[rows, :] = kk
        bb_ref[rows, :] = bcs
        qq = pr_ref[rows, c_q:c_q + hk] * (HGRN_DK ** -0.5)
        q_dec = (qq * jnp.exp(bcs)).astype(BF16)
        k_dec = (kk * jnp.exp(btot - bcs)).astype(BF16)
        blk_dec = jnp.exp(btot)
        gate = _silu(pr_ref[rows, c_g:c_g + hk])
        for h in range(HGRN_HEADS):
            hs = slice(h * HGRN_DK, (h + 1) * HGRN_DK)
            qh = qq[:, hs]
            bh = bcs[:, hs]
            prods = []
            for delta in range(SUB):
                ks = kb_ref[r0 - delta:r0 - delta + CHUNK, hs]
                bs = bb_ref[r0 - delta:r0 - delta + CHUNK, hs]
                ok = row_in_sub >= delta
                pd = jnp.where(ok, qh * ks * jnp.exp(jnp.where(ok, bh - bs, 0.0)), 0.0)
                prods.append(pd.astype(BF16))
            sc = _dot(jnp.concatenate(prods, axis=0), ones_k)
            o_h = jnp.zeros((CHUNK, HGRN_DK), F32)
            for delta in range(SUB):
                vs = pr_ref[r0 - delta:r0 - delta + CHUNK, c_i + h * HGRN_DK:c_i + (h + 1) * HGRN_DK]
                o_h = o_h + sc[delta * CHUNK:(delta + 1) * CHUNK, :] * vs
            st = sth_ref[h]
            inter = []
            for i in range(CHUNK // SUB):
                rs = slice(i * SUB, (i + 1) * SUB)
                inter.append(_dot_nt(q_dec[rs, hs], st.astype(BF16)))
                vi = pr_ref[r0 + i * SUB:r0 + (i + 1) * SUB, c_i + h * HGRN_DK:c_i + (h + 1) * HGRN_DK]
                st = st * blk_dec[i * SUB:i * SUB + 1, hs] + _dot_tn(vi.astype(BF16), k_dec[rs, hs])
            sth_ref[h] = st
            o_h = o_h + jnp.concatenate(inter, axis=0)
            o_h = _rms(o_h) * onw_ref[:, hs] * gate[:, hs]
            y_ref[c * CHUNK:(c + 1) * CHUNK, hs] = o_h.astype(BF16)

        conv = jnp.broadcast_to(cb_ref[...], (CHUNK, cw_ref.shape[1]))
        for j in range(taps):
            lo = r0 - (taps - 1) + j
            conv = conv + cw_ref[j:j + 1, :] * pr_ref[lo:lo + CHUNK, c_x:c_x + sd + 2 * gn]
        xbc = _silu(conv)
        xs = xbc[:, 0:sd]
        dt = jnp.logaddexp(pr_ref[rows, c_dt:c_dt + LANES] + dtb_ref[...], 0.0)
        da = dt * a_neg
        cs = _dot01(tri_chunk, da)
        cs_last = cs[CHUNK - 1:CHUNK, :]
        dt_x = _dot01_r(dt, expand)
        ecs_x = _dot01_r(jnp.exp(cs), expand)
        end_x = _dot01_r(jnp.exp(cs_last - cs), expand)
        xc = xs * dt_x
        xcd = (xc * end_x).astype(BF16)
        xcb = xc.astype(BF16)
        d0, d1, d2 = _split3(da)
        cs_rows = _dot_tn(d0, tri_chunk_t) + _dot_tn(d1, tri_chunk_t) + _dot_tn(d2, tri_chunk_t)
        ys = []
        for g in range(SSD_GROUPS):
            bm = xbc[:, sd + g * SSD_STATE:sd + (g + 1) * SSD_STATE].astype(BF16)
            cm = xbc[:, sd + gn + g * SSD_STATE:sd + gn + (g + 1) * SSD_STATE].astype(BF16)
            cbm = _dot_nt(cm, bm)
            gl = slice(g * hp, (g + 1) * hp)
            st = sts_ref[g]
            y_off = _dot(cm, st.astype(BF16)) * ecs_x[:, gl]
            sts_ref[g] = st * ecs_x[CHUNK - 1:CHUNK, gl] + _dot_tn(bm, xcd[:, gl])
            for j in range(SSD_HEADS // SSD_GROUPS):
                h = g * (SSD_HEADS // SSD_GROUPS) + j
                diff = cs[:, h:h + 1] - cs_rows[h:h + 1, :]
                dec = jnp.where(causal, jnp.exp(jnp.where(causal, diff, 0.0)), 0.0)
                hl = slice(h * SSD_HEADDIM, (h + 1) * SSD_HEADDIM)
                yd = _dot((cbm * dec).astype(BF16), xcb[:, hl])
                ys.append(yd + y_off[:, j * SSD_HEADDIM:(j + 1) * SSD_HEADDIM])
        y = jnp.concatenate(ys, axis=-1) + xs * dsk_ref[...]
        y = y * _silu(pr_ref[rows, c_z:c_z + sd])
        yg = []
        for g in range(SSD_GROUPS):
            yg.append(_rms(y[:, g * hp:(g + 1) * hp]))
        y = jnp.concatenate(yg, axis=-1) * snw_ref[...]
        y_ref[c * CHUNK:(c + 1) * CHUNK, hk:hk + sd] = y.astype(BF16)

    pr_ref[0:P, :] = pr_ref[tb:tb + P, :]
    kb_ref[0:P, :] = kb_ref[tb:tb + P, :]
    bb_ref[0:P, :] = bb_ref[tb:tb + P, :]

    o_ref[0] = x + _dot(y_ref[...], wout_ref[...])


def _mixer(x, norm_w, w_in, lb_logits, out_norm_w, conv_w, conv_b, dt_bias, a_log, d_skip, ssd_norm_w,
           w_out, e, *, tb=256):
    b, t, d = x.shape
    n_even, hk = lb_logits.shape
    sd = SSD_HEADS * SSD_HEADDIM
    ab_in = w_in.shape[1]
    pw = -(-ab_in // LANES) * LANES
    w_in_p = jnp.pad(w_in, ((0, 0), (0, pw - ab_in))).astype(BF16)
    pad8 = lambda a: jnp.pad(a.reshape(1, -1), ((0, 0), (0, LANES - a.shape[-1])))
    cc = conv_w.shape[1]
    return pl.pallas_call(
        functools.partial(_mix_body, e=e),
        grid=(b, t // tb),
        in_specs=[
            pl.BlockSpec((1, tb, d), lambda i, j: (i, j, 0)),
            _resident((1, d)),
            _resident((d, pw)),
            _resident((n_even, hk)),
            _resident((1, hk)),
            _resident((conv_w.shape[0], cc)),
            _resident((1, cc)),
            _resident((1, LANES)),
            _resident((1, LANES)),
            _resident((1, sd)),
            _resident((1, sd)),
            _resident((hk + sd, d)),
        ],
        out_specs=pl.BlockSpec((1, tb, d), lambda i, j: (i, j, 0)),
        out_shape=jax.ShapeDtypeStruct((b, t, d), F32),
        scratch_shapes=[
            pltpu.VMEM((PAD_ROWS + tb, pw), F32),
            pltpu.VMEM((PAD_ROWS + tb, hk), F32),
            pltpu.VMEM((PAD_ROWS + tb, hk), F32),
            pltpu.VMEM((HGRN_HEADS, HGRN_DK, HGRN_DK), F32),
            pltpu.VMEM((SSD_GROUPS, SSD_STATE, sd // SSD_GROUPS), F32),
            pltpu.VMEM((tb, hk + sd), BF16),
        ],
        compiler_params=_params(2),
        name="hgrn_ssd_mixer",
    )(x, norm_w.reshape(1, d), w_in_p, lb_logits, out_norm_w.reshape(1, hk), conv_w, conv_b.reshape(1, cc),
      pad8(dt_bias), pad8(a_log), jnp.repeat(d_skip, SSD_HEADDIM).reshape(1, sd), ssd_norm_w.reshape(1, sd),
      w_out.astype(BF16))


def kernel(x, mem, mem_norm_w, norm_mix_w, ab_w_in, hgrn_lb_logits, hgrn_out_norm_w, ssd_conv_w, ssd_conv_b,
           ssd_dt_bias, ssd_a_log, ssd_d, ssd_norm_w, ab_w_out, cv_w_pw1, cv_b_pw1, cv_w_dw, cv_b_dw, cv_ln_w,
           cv_ln_b, cv_w_pw2, cv_b_pw2, norm_xattn_w, xattn_wq, xattn_wk, xattn_wv, xattn_wo, norm_mlp_w,
           mlp_w1, mlp_w2, final_norm_w):
    b, t, d = x.shape
    depth = norm_mix_w.shape[0]
    k_all, v_all = _memory_kv(mem, mem_norm_w, xattn_wk, xattn_wv)
    for layer in range(depth):
        if layer % 2 == 0:
            e = layer // 2
            x = _mixer(x, norm_mix_w[layer], ab_w_in[e], hgrn_lb_logits, hgrn_out_norm_w[e], ssd_conv_w[e],
                       ssd_conv_b[e], ssd_dt_bias[e], ssd_a_log[e], ssd_d[e], ssd_norm_w[e], ab_w_out[e], e)
        else:
            o = layer // 2
            x = _conformer(x, norm_mix_w[layer], cv_w_pw1[o], cv_b_pw1[o], cv_w_dw[o], cv_b_dw[o], cv_ln_w[o],
                           cv_ln_b[o], cv_w_pw2[o], cv_b_pw2[o])
        x = _cross_attention(x, norm_xattn_w[layer], xattn_wq[layer], k_all, v_all, xattn_wo[layer], layer)
        x = _mlp(x.reshape(b * t, d), norm_mlp_w[layer], mlp_w1[layer], mlp_w2[layer], final_norm_w,
                 final=(layer == depth - 1)).reshape(b, t, d)
    return x
```

```python
import functools

import jax
import jax.numpy as jnp
from jax import lax
from jax.experimental import pallas as pl
from jax.experimental.pallas import tpu as pltpu

F32 = jnp.float32
BF16 = jnp.bfloat16

EPS = 1e-6
CHUNK = 64
SUB = 16
LANES = 128
SUBLANES = 8
PAD_ROWS = 16

HGRN_HEADS = 4
HGRN_DK = 128
SSD_HEADS = 8
SSD_HEADDIM = 64
SSD_GROUPS = 2
SSD_STATE = 128
XATTN_HEADS = 4

VMEM_LIMIT_BYTES = 56 * 1024 * 1024


def _rms(x):
    return x * lax.rsqrt(jnp.mean(x * x, axis=-1, keepdims=True) + EPS)


def _sigmoid(x):
    return 1.0 / (1.0 + jnp.exp(-x))


def _silu(x):
    return x * _sigmoid(x)


def _dot(a, b):
    return jnp.dot(a, b, preferred_element_type=F32)


def _dot_nt(a, b):
    return lax.dot_general(a, b, (((1,), (1,)), ((), ())), preferred_element_type=F32)


def _dot_tn(a, b):
    return lax.dot_general(a, b, (((0,), (0,)), ((), ())), preferred_element_type=F32)


def _split3(a):
    a0 = a.astype(BF16)
    r1 = a - a0.astype(F32)
    a1 = r1.astype(BF16)
    a2 = (r1 - a1.astype(F32)).astype(BF16)
    return a0, a1, a2


def _dot01(m01, a):
    a0, a1, a2 = _split3(a)
    return _dot(m01, a0) + _dot(m01, a1) + _dot(m01, a2)


def _dot01_r(a, m01):
    a0, a1, a2 = _split3(a)
    return _dot(a0, m01) + _dot(a1, m01) + _dot(a2, m01)


def _resident(shape):
    nd = len(shape)
    return pl.BlockSpec(shape, lambda *_: (0,) * nd, pipeline_mode=pl.Buffered(1))


def _params(n_axes):
    return pltpu.CompilerParams(dimension_semantics=("arbitrary",) * n_axes,
                                vmem_limit_bytes=VMEM_LIMIT_BYTES)


def _kv_body(mem_ref, mw_ref, wk_ref, wv_ref, k_ref, v_ref):
    m = mem_ref[0]
    mn = (_rms(m) * mw_ref[...]).astype(BF16)
    k_ref[0, 0] = _dot(mn, wk_ref[0].astype(BF16)).astype(BF16)
    v_ref[0, 0] = _dot(mn, wv_ref[0].astype(BF16)).astype(BF16)


def _memory_kv(mem, mem_norm_w, wk, wv):
    depth, d, _ = wk.shape
    b, m, _ = mem.shape
    out = jax.ShapeDtypeStruct((depth, b, m, d), BF16)
    return pl.pallas_call(
        _kv_body,
        grid=(depth, b),
        in_specs=[
            pl.BlockSpec((1, m, d), lambda l, i: (i, 0, 0)),
            pl.BlockSpec((1, d), lambda l, i: (0, 0)),
            pl.BlockSpec((1, d, d), lambda l, i: (l, 0, 0)),
            pl.BlockSpec((1, d, d), lambda l, i: (l, 0, 0)),
        ],
        out_specs=[
            pl.BlockSpec((1, 1, m, d), lambda l, i: (l, i, 0, 0)),
            pl.BlockSpec((1, 1, m, d), lambda l, i: (l, i, 0, 0)),
        ],
        out_shape=[out, out],
        compiler_params=_params(2),
        name="memory_kv",
    )(mem, mem_norm_w.reshape(1, d), wk, wv)


def _mlp_body(x_ref, nw_ref, w1_ref, w2_ref, fw_ref, o_ref, *, ff_chunk, final):
    x = x_ref[...]
    hb = (_rms(x) * nw_ref[...]).astype(BF16)
    acc = x
    for c in range(w1_ref.shape[1] // ff_chunk):
        a = _dot(hb, w1_ref[:, c * ff_chunk:(c + 1) * ff_chunk])
        a = jnp.square(jnp.maximum(a, 0.0)).astype(BF16)
        acc = acc + _dot(a, w2_ref[c * ff_chunk:(c + 1) * ff_chunk, :])
    if final:
        acc = _rms(acc) * fw_ref[...]
    o_ref[...] = acc


def _mlp(x2, norm_w, w1, w2, final_w, *, final, tb=512, ff_chunk=1024):
    n, d = x2.shape
    dff = w1.shape[1]
    return pl.pallas_call(
        functools.partial(_mlp_body, ff_chunk=ff_chunk, final=final),
        grid=(n // tb,),
        in_specs=[
            pl.BlockSpec((tb, d), lambda i: (i, 0)),
            _resident((1, d)),
            _resident((d, dff)),
            _resident((dff, d)),
            _resident((1, d)),
        ],
        out_specs=pl.BlockSpec((tb, d), lambda i: (i, 0)),
        out_shape=jax.ShapeDtypeStruct((n, d), F32),
        compiler_params=_params(1),
        name="sq_relu_mlp",
    )(x2, norm_w.reshape(1, d), w1.astype(BF16), w2.astype(BF16), final_w.reshape(1, d))


def _attn_body(x_ref, nw_ref, wq_ref, k_ref, v_ref, wo_ref, o_ref, *, heads):
    x = x_ref[0]
    d = x.shape[-1]
    hd = d // heads
    hb = (_rms(x) * nw_ref[...]).astype(BF16)
    q = (_dot(hb, wq_ref[...]) * (hd ** -0.5)).astype(BF16)
    outs = []
    for h in range(heads):
        sl = slice(h * hd, (h + 1) * hd)
        s = _dot_nt(q[:, sl], k_ref[0, 0, :, sl])
        p = jnp.exp(s - jnp.max(s, axis=-1, keepdims=True))
        den = jnp.sum(p, axis=-1, keepdims=True)
        oh = _dot(p.astype(BF16), v_ref[0, 0, :, sl]) * (1.0 / den)
        outs.append(oh.astype(BF16))
    o = jnp.concatenate(outs, axis=-1)
    o_ref[0] = x + _dot(o, wo_ref[...])


def _cross_attention(x, norm_w, wq, k_all, v_all, wo, layer, *, tb=512):
    b, t, d = x.shape
    m = k_all.shape[2]
    return pl.pallas_call(
        functools.partial(_attn_body, heads=XATTN_HEADS),
        grid=(b, t // tb),
        in_specs=[
            pl.BlockSpec((1, tb, d), lambda i, j: (i, j, 0)),
            _resident((1, d)),
            _resident((d, d)),
            pl.BlockSpec((1, 1, m, d), lambda i, j: (layer, i, 0, 0)),
            pl.BlockSpec((1, 1, m, d), lambda i, j: (layer, i, 0, 0)),
            _resident((d, d)),
        ],
        out_specs=pl.BlockSpec((1, tb, d), lambda i, j: (i, j, 0)),
        out_shape=jax.ShapeDtypeStruct((b, t, d), F32),
        compiler_params=_params(2),
        name="memory_xattn",
    )(x, norm_w.reshape(1, d), wq.astype(BF16), k_all, v_all, wo.astype(BF16))


def _conf_body(x_ref, nw_ref, w1_ref, b1_ref, wdw_ref, bdw_ref, lnw_ref, lnb_ref, w2_ref, b2_ref,
               o_ref, u_ref, c_ref, *, taps, hist, t_step):
    tb, d = x_ref.shape[1], x_ref.shape[2]
    n_slab = d // LANES
    p_u = u_ref.shape[0] // n_slab
    p_c = c_ref.shape[0] // n_slab

    @pl.when(pl.program_id(1) == 0)
    def _():
        for s in range(n_slab):
            u_ref[s * p_u:s * p_u + hist, :] = jnp.zeros((hist, LANES), F32)

    x = x_ref[0]
    hb = (_rms(x) * nw_ref[...]).astype(BF16)
    ag = _dot(hb, w1_ref[...]) + b1_ref[...]
    u = ag[:, :d] * _sigmoid(ag[:, d:])
    for s in range(n_slab):
        u_ref[s * p_u + hist:s * p_u + hist + tb, :] = u[:, s * LANES:(s + 1) * LANES]

    w_taps = [wdw_ref[j] for j in range(taps)]
    bias = bdw_ref[...]

    def conv_step(i, carry):
        t0 = pl.multiple_of(i * t_step, t_step)
        accs = [bias] * t_step
        for k in range(t_step + taps - 1):
            uk = u_ref[pl.ds(t0 + hist - (taps - 1) + k, n_slab, stride=p_u), :]
            for tt in range(t_step):
                if 0 <= k - tt < taps:
                    accs[tt] = accs[tt] + w_taps[k - tt] * uk
        for tt in range(t_step):
            c_ref[pl.ds(t0 + tt, n_slab, stride=p_c), :] = accs[tt]
        return carry

    lax.fori_loop(0, tb // t_step, conv_step, 0)
    for s in range(n_slab):
        u_ref[s * p_u:s * p_u + hist, :] = u_ref[s * p_u + tb:s * p_u + tb + hist, :]

    c = jnp.concatenate([c_ref[s * p_c:s * p_c + tb, :] for s in range(n_slab)], axis=-1)
    mu = jnp.mean(c, axis=-1, keepdims=True)
    cc = c - mu
    var = jnp.mean(cc * cc, axis=-1, keepdims=True)
    yn = cc * lax.rsqrt(var + EPS) * lnw_ref[...] + lnb_ref[...]
    ys = _silu(yn).astype(BF16)
    o_ref[0] = x + _dot(ys, w2_ref[...]) + b2_ref[...]


def _conformer(x, norm_w, w_pw1, b_pw1, w_dw, b_dw, ln_w, ln_b, w_pw2, b_pw2, *, tb=512, t_step=8):
    b, t, d = x.shape
    taps = w_dw.shape[0]
    n_slab = d // LANES
    assert n_slab == SUBLANES
    hist = -(-(taps - 1) // SUBLANES) * SUBLANES
    p_u = hist + tb + 4
    p_c = tb + SUBLANES
    return pl.pallas_call(
        functools.partial(_conf_body, taps=taps, hist=hist, t_step=t_step),
        grid=(b, t // tb),
        in_specs=[
            pl.BlockSpec((1, tb, d), lambda i, j: (i, j, 0)),
            _resident((1, d)),
            _resident((d, 2 * d)),
            _resident((1, 2 * d)),
            _resident((taps, n_slab, LANES)),
            _resident((n_slab, LANES)),
            _resident((1, d)),
            _resident((1, d)),
            _resident((d, d)),
            _resident((1, d)),
        ],
        out_specs=pl.BlockSpec((1, tb, d), lambda i, j: (i, j, 0)),
        out_shape=jax.ShapeDtypeStruct((b, t, d), F32),
        scratch_shapes=[pltpu.VMEM((n_slab * p_u, LANES), F32), pltpu.VMEM((n_slab * p_c, LANES), F32)],
        compiler_params=_params(2),
        name="conformer_conv",
    )(x, norm_w.reshape(1, d), w_pw1.astype(BF16), b_pw1.reshape(1, 2 * d), w_dw.reshape(taps, n_slab, LANES),
      b_dw.reshape(n_slab, LANES), ln_w.reshape(1, d), ln_b.reshape(1, d), w_pw2.astype(BF16), b_pw2.reshape(1, d))


def _iota2(shape, axis):
    return lax.broadcasted_iota(jnp.int32, shape, axis)


def _mix_body(x_ref, nw_ref, win_ref, lbl_ref, onw_ref, cw_ref, cb_ref, dtb_ref, alog_ref, dsk_ref,
              snw_ref, wout_ref, o_ref, pr_ref, kb_ref, bb_ref, sth_ref, sts_ref, y_ref, *, e):
    tb, d = x_ref.shape[1], x_ref.shape[2]
    hk = HGRN_HEADS * HGRN_DK
    sd = SSD_HEADS * SSD_HEADDIM
    gn = SSD_GROUPS * SSD_STATE
    c_q, c_f, c_i, c_g = 0, hk, 2 * hk, 3 * hk
    c_z = 4 * hk
    c_x = c_z + sd
    c_b = c_x + sd
    c_c = c_b + gn
    c_dt = c_c + gn
    hp = SSD_HEADS // SSD_GROUPS * SSD_HEADDIM
    taps = cw_ref.shape[0]
    P = PAD_ROWS

    @pl.when(pl.program_id(1) == 0)
    def _():
        pr_ref[0:P, :] = jnp.zeros((P, pr_ref.shape[1]), F32)
        kb_ref[0:P, :] = jnp.zeros((P, hk), F32)
        bb_ref[0:P, :] = jnp.zeros((P, hk), F32)
        sth_ref[...] = jnp.zeros(sth_ref.shape, F32)
        sts_ref[...] = jnp.zeros(sts_ref.shape, F32)

    x = x_ref[0]
    hb = (_rms(x) * nw_ref[...]).astype(BF16)
    pr_ref[P:P + tb, :] = _dot(hb, win_ref[...])

    lg = lbl_ref[...]
    ex = jnp.exp(lg - jnp.max(lg, axis=0, keepdims=True))
    probs = ex / jnp.sum(ex, axis=0, keepdims=True)
    lb = jnp.sum(probs[0:e + 1], axis=0, keepdims=True) - probs[0:1]

    r64 = _iota2((CHUNK, CHUNK), 0)
    c64 = _iota2((CHUNK, CHUNK), 1)
    same_sub = (r64 // SUB) == (c64 // SUB)
    tri_sub = jnp.where(same_sub & (c64 <= r64), 1.0, 0.0).astype(BF16)
    ones_sub = jnp.where(same_sub, 1.0, 0.0).astype(BF16)
    tri_chunk = jnp.where(c64 <= r64, 1.0, 0.0).astype(BF16)
    tri_chunk_t = jnp.where(r64 <= c64, 1.0, 0.0).astype(BF16)
    causal = c64 <= r64
    expand = jnp.where(_iota2((LANES, sd), 1) // SSD_HEADDIM == _iota2((LANES, sd), 0), 1.0, 0.0).astype(BF16)
    ones_k = jnp.ones((HGRN_DK, HGRN_DK), BF16)
    row_in_sub = _iota2((CHUNK, HGRN_DK), 0) % SUB

    a_neg = -jnp.exp(alog_ref[...])

    for c in range(tb // CHUNK):
        r0 = P + c * CHUNK
        rows = slice(r0, r0 + CHUNK)

        fr = pr_ref[rows, c_f:c_f + hk]
        lf = jnp.log(lb + (1.0 - lb) * _sigmoid(fr))
        kk = (1.0 - lb) * _sigmoid(-fr)
        bcs = _dot01(tri_sub, lf)
        btot = _dot01(ones_sub, lf)
        kb_ref[rows, :] = kk
        bb_ref[rows, :] = bcs
        qq = pr_ref[rows, c_q:c_q + hk] * (HGRN_DK ** -0.5)
        q_dec = (qq * jnp.exp(bcs)).astype(BF16)
        k_dec = (kk * jnp.exp(btot - bcs)).astype(BF16)
        blk_dec = jnp.exp(btot)
        gate = _silu(pr_ref[rows, c_g:c_g + hk])
        for h in range(HGRN_HEADS):
            hs = slice(h * HGRN_DK, (h + 1) * HGRN_DK)
            qh = qq[:, hs]
            bh = bcs[:, hs]
            prods = []
            for delta in range(SUB):
                ks = kb_ref[r0 - delta:r0 - delta + CHUNK, hs]
                bs = bb_ref[r0 - delta:r0 - delta + CHUNK, hs]
                ok = row_in_sub >= delta
                pd = jnp.where(ok, qh * ks * jnp.exp(jnp.where(ok, bh - bs, 0.0)), 0.0)
                prods.append(pd.astype(BF16))
            sc = _dot(jnp.concatenate(prods, axis=0), ones_k)
            o_h = jnp.zeros((CHUNK, HGRN_DK), F32)
            for delta in range(SUB):
                vs = pr_ref[r0 - delta:r0 - delta + CHUNK, c_i + h * HGRN_DK:c_i + (h + 1) * HGRN_DK]
                o_h = o_h + sc[delta * CHUNK:(delta + 1) * CHUNK, :] * vs
            st = sth_ref[h]
            inter = []
            for i in range(CHUNK // SUB):
                rs = slice(i * SUB, (i + 1) * SUB)
                inter.append(_dot_nt(q_dec[rs, hs], st.astype(BF16)))
                vi = pr_ref[r0 + i * SUB:r0 + (i + 1) * SUB, c_i + h * HGRN_DK:c_i + (h + 1) * HGRN_DK]
                st = st * blk_dec[i * SUB:i * SUB + 1, hs] + _dot_tn(vi.astype(BF16), k_dec[rs, hs])
            sth_ref[h] = st
            o_h = o_h + jnp.concatenate(inter, axis=0)
            o_h = _rms(o_h) * onw_ref[:, hs] * gate[:, hs]
            y_ref[c * CHUNK:(c + 1) * CHUNK, hs] = o_h.astype(BF16)

        conv = jnp.broadcast_to(cb_ref[...], (CHUNK, cw_ref.shape[1]))
        for j in range(taps):
            lo = r0 - (taps - 1) + j
            conv = conv + cw_ref[j:j + 1, :] * pr_ref[lo:lo + CHUNK, c_x:c_x + sd + 2 * gn]
        xbc = _silu(conv)
        xs = xbc[:, 0:sd]
        dt = jnp.logaddexp(pr_ref[rows, c_dt:c_dt + LANES] + dtb_ref[...], 0.0)
        da = dt * a_neg
        cs = _dot01(tri_chunk, da)
        cs_last = cs[CHUNK - 1:CHUNK, :]
        dt_x = _dot01_r(dt, expand)
        ecs_x = _dot01_r(jnp.exp(cs), expand)
        end_x = _dot01_r(jnp.exp(cs_last - cs), expand)
        xc = xs * dt_x
        xcd = (xc * end_x).astype(BF16)
        xcb = xc.astype(BF16)
        d0, d1, d2 = _split3(da)
        cs_rows = _dot_tn(d0, tri_chunk_t) + _dot_tn(d1, tri_chunk_t) + _dot_tn(d2, tri_chunk_t)
        ys = []
        for g in range(SSD_GROUPS):
            bm = xbc[:, sd + g * SSD_STATE:sd + (g + 1) * SSD_STATE].astype(BF16)
            cm = xbc[:, sd + gn + g * SSD_STATE:sd + gn + (g + 1) * SSD_STATE].astype(BF16)
            cbm = _dot_nt(cm, bm)
            gl = slice(g * hp, (g + 1) * hp)
            st = sts_ref[g]
            y_off = _dot(cm, st.astype(BF16)) * ecs_x[:, gl]
            sts_ref[g] = st * ecs_x[CHUNK - 1:CHUNK, gl] + _dot_tn(bm, xcd[:, gl])
            for j in range(SSD_HEADS // SSD_GROUPS):
                h = g * (SSD_HEADS // SSD_GROUPS) + j
                diff = cs[:, h:h + 1] - cs_rows[h:h + 1, :]
                dec = jnp.where(causal, jnp.exp(jnp.where(causal, diff, 0.0)), 0.0)
                hl = slice(h * SSD_HEADDIM, (h + 1) * SSD_HEADDIM)
                yd = _dot((cbm * dec).astype(BF16), xcb[:, hl])
                ys.append(yd + y_off[:, j * SSD_HEADDIM:(j + 1) * SSD_HEADDIM])
        y = jnp.concatenate(ys, axis=-1) + xs * dsk_ref[...]
        y = y * _silu(pr_ref[rows, c_z:c_z + sd])
        yg = []
        for g in range(SSD_GROUPS):
            yg.append(_rms(y[:, g * hp:(g + 1) * hp]))
        y = jnp.concatenate(yg, axis=-1) * snw_ref[...]
        y_ref[c * CHUNK:(c + 1) * CHUNK, hk:hk + sd] = y.astype(BF16)

    pr_ref[0:P, :] = pr_ref[tb:tb + P, :]
    kb_ref[0:P, :] = kb_ref[tb:tb + P, :]
    bb_ref[0:P, :] = bb_ref[tb:tb + P, :]

    o_ref[0] = x + _dot(y_ref[...], wout_ref[...])


def _mixer(x, norm_w, w_in, lb_logits, out_norm_w, conv_w, conv_b, dt_bias, a_log, d_skip, ssd_norm_w,
           w_out, e, *, tb=256):
    b, t, d = x.shape
    n_even, hk = lb_logits.shape
    sd = SSD_HEADS * SSD_HEADDIM
    ab_in = w_in.shape[1]
    pw = -(-ab_in // LANES) * LANES
    w_in_p = jnp.pad(w_in, ((0, 0), (0, pw - ab_in))).astype(BF16)
    pad8 = lambda a: jnp.pad(a.reshape(1, -1), ((0, 0), (0, LANES - a.shape[-1])))
    cc = conv_w.shape[1]
    return pl.pallas_call(
        functools.partial(_mix_body, e=e),
        grid=(b, t // tb),
        in_specs=[
            pl.BlockSpec((1, tb, d), lambda i, j: (i, j, 0)),
            _resident((1, d)),
            _resident((d, pw)),
            _resident((n_even, hk)),
            _resident((1, hk)),
            _resident((conv_w.shape[0], cc)),
            _resident((1, cc)),
            _resident((1, LANES)),
            _resident((1, LANES)),
            _resident((1, sd)),
            _resident((1, sd)),
            _resident((hk + sd, d)),
        ],
        out_specs=pl.BlockSpec((1, tb, d), lambda i, j: (i, j, 0)),
        out_shape=jax.ShapeDtypeStruct((b, t, d), F32),
        scratch_shapes=[
            pltpu.VMEM((PAD_ROWS + tb, pw), F32),
            pltpu.VMEM((PAD_ROWS + tb, hk), F32),
            pltpu.VMEM((PAD_ROWS + tb, hk), F32),
            pltpu.VMEM((HGRN_HEADS, HGRN_DK, HGRN_DK), F32),
            pltpu.VMEM((SSD_GROUPS, SSD_STATE, sd // SSD_GROUPS), F32),
            pltpu.VMEM((tb, hk + sd), BF16),
        ],
        compiler_params=_params(2),
        name="hgrn_ssd_mixer",
    )(x, norm_w.reshape(1, d), w_in_p, lb_logits, out_norm_w.reshape(1, hk), conv_w, conv_b.reshape(1, cc),
      pad8(dt_bias), pad8(a_log), jnp.repeat(d_skip, SSD_HEADDIM).reshape(1, sd), ssd_norm_w.reshape(1, sd),
      w_out.astype(BF16))


def kernel(x, mem, mem_norm_w, norm_mix_w, ab_w_in, hgrn_lb_logits, hgrn_out_norm_w, ssd_conv_w, ssd_conv_b,
           ssd_dt_bias, ssd_a_log, ssd_d, ssd_norm_w, ab_w_out, cv_w_pw1, cv_b_pw1, cv_w_dw, cv_b_dw, cv_ln_w,
           cv_ln_b, cv_w_pw2, cv_b_pw2, norm_xattn_w, xattn_wq, xattn_wk, xattn_wv, xattn_wo, norm_mlp_w,
           mlp_w1, mlp_w2, final_norm_w):
    b, t, d = x.shape
    depth = norm_mix_w.shape[0]
    k_all, v_all = _memory_kv(mem, mem_norm_w, xattn_wk, xattn_wv)
    for layer in range(depth):
        if layer % 2 == 0:
            e = layer // 2
            x = _mixer(x, norm_mix_w[layer], ab_w_in[e], hgrn_lb_logits, hgrn_out_norm_w[e], ssd_conv_w[e],
                       ssd_conv_b[e], ssd_dt_bias[e], ssd_a_log[e], ssd_d[e], ssd_norm_w[e], ab_w_out[e], e)
        else:
            o = layer // 2
            x = _conformer(x, norm_mix_w[layer], cv_w_pw1[o], cv_b_pw1[o], cv_w_dw[o], cv_b_dw[o], cv_ln_w[o],
                           cv_ln_b[o], cv_w_pw2[o], cv_b_pw2[o])
        x = _cross_attention(x, norm_xattn_w[layer], xattn_wq[layer], k_all, v_all, xattn_wo[layer], layer)
        x = _mlp(x.reshape(b * t, d), norm_mlp_w[layer], mlp_w1[layer], mlp_w2[layer], final_norm_w,
                 final=(layer == depth - 1)).reshape(b, t, d)
    return x
```

```python
import functools

import jax
import jax.numpy as jnp
from jax import lax
from jax.experimental import pallas as pl
from jax.experimental.pallas import tpu as pltpu

F32 = jnp.float32
BF16 = jnp.bfloat16

EPS = 1e-6
CHUNK = 64
SUB = 16
LANES = 128
SUBLANES = 8
PAD_ROWS = 8

HGRN_HEADS = 4
HGRN_DK = 128
SSD_HEADS = 8
SSD_HEADDIM = 64
SSD_GROUPS = 2
SSD_STATE = 128
XATTN_HEADS = 4

VMEM_LIMIT_BYTES = 56 * 1024 * 1024


def _rms(x):
    return x * lax.rsqrt(jnp.mean(x * x, axis=-1, keepdims=True) + EPS)


def _sigmoid(x):
    return 1.0 / (1.0 + jnp.exp(-x))


def _silu(x):
    return x * _sigmoid(x)


def _dot(a, b):
    return jnp.dot(a, b, preferred_element_type=F32)


def _dot_nt(a, b):
    return lax.dot_general(a, b, (((1,), (1,)), ((), ())), preferred_element_type=F32)


def _dot_tn(a, b):
    return lax.dot_general(a, b, (((0,), (0,)), ((), ())), preferred_element_type=F32)


def _split3(a):
    a0 = a.astype(BF16)
    r1 = a - a0.astype(F32)
    a1 = r1.astype(BF16)
    a2 = (r1 - a1.astype(F32)).astype(BF16)
    return a0, a1, a2


def _dot01(m01, a):
    a0, a1, a2 = _split3(a)
    return _dot(m01, a0) + _dot(m01, a1) + _dot(m01, a2)


def _resident(shape):
    nd = len(shape)
    return pl.BlockSpec(shape, lambda *_: (0,) * nd, pipeline_mode=pl.Buffered(1))


def _params(n_axes):
    return pltpu.CompilerParams(dimension_semantics=("arbitrary",) * n_axes,
                                vmem_limit_bytes=VMEM_LIMIT_BYTES)


def _kv_body(mem_ref, mw_ref, wk_ref, wv_ref, k_ref, v_ref):
    m = mem_ref[0]
    mn = (_rms(m) * mw_ref[...]).astype(BF16)
    k_ref[0, 0] = _dot(mn, wk_ref[0].astype(BF16)).astype(BF16)
    v_ref[0, 0] = _dot(mn, wv_ref[0].astype(BF16)).astype(BF16)


def _memory_kv(mem, mem_norm_w, wk, wv):
    depth, d, _ = wk.shape
    b, m, _ = mem.shape
    out = jax.ShapeDtypeStruct((depth, b, m, d), BF16)
    return pl.pallas_call(
        _kv_body,
        grid=(depth, b),
        in_specs=[
            pl.BlockSpec((1, m, d), lambda l, i: (i, 0, 0)),
            pl.BlockSpec((1, d), lambda l, i: (0, 0)),
            pl.BlockSpec((1, d, d), lambda l, i: (l, 0, 0)),
            pl.BlockSpec((1, d, d), lambda l, i: (l, 0, 0)),
        ],
        out_specs=[
            pl.BlockSpec((1, 1, m, d), lambda l, i: (l, i, 0, 0)),
            pl.BlockSpec((1, 1, m, d), lambda l, i: (l, i, 0, 0)),
        ],
        out_shape=[out, out],
        compiler_params=_params(2),
        name="memory_kv",
    )(mem, mem_norm_w.reshape(1, d), wk, wv)


def _mlp_body(x_ref, nw_ref, w1_ref, w2_ref, fw_ref, o_ref, *, ff_chunk, final):
    x = x_ref[...]
    hb = (_rms(x) * nw_ref[...]).astype(BF16)
    acc = x
    for c in range(w1_ref.shape[1] // ff_chunk):
        a = _dot(hb, w1_ref[:, c * ff_chunk:(c + 1) * ff_chunk])
        a = jnp.square(jnp.maximum(a, 0.0)).astype(BF16)
        acc = acc + _dot(a, w2_ref[c * ff_chunk:(c + 1) * ff_chunk, :])
    if final:
        acc = _rms(acc) * fw_ref[...]
    o_ref[...] = acc


def _mlp(x2, norm_w, w1, w2, final_w, *, final, tb=512, ff_chunk=1024):
    n, d = x2.shape
    dff = w1.shape[1]
    return pl.pallas_call(
        functools.partial(_mlp_body, ff_chunk=ff_chunk, final=final),
        grid=(n // tb,),
        in_specs=[
            pl.BlockSpec((tb, d), lambda i: (i, 0)),
            _resident((1, d)),
            _resident((d, dff)),
            _resident((dff, d)),
            _resident((1, d)),
        ],
        out_specs=pl.BlockSpec((tb, d), lambda i: (i, 0)),
        out_shape=jax.ShapeDtypeStruct((n, d), F32),
        compiler_params=_params(1),
        name="sq_relu_mlp",
    )(x2, norm_w.reshape(1, d), w1.astype(BF16), w2.astype(BF16), final_w.reshape(1, d))


def _attn_body(x_ref, nw_ref, wq_ref, k_ref, v_ref, wo_ref, o_ref, *, heads):
    x = x_ref[0]
    d = x.shape[-1]
    hd = d // heads
    hb = (_rms(x) * nw_ref[...]).astype(BF16)
    q = (_dot(hb, wq_ref[...]) * (hd ** -0.5)).astype(BF16)
    outs = []
    for h in range(heads):
        sl = slice(h * hd, (h + 1) * hd)
        s = _dot_nt(q[:, sl], k_ref[0, 0, :, sl])
        p = jnp.exp(s - jnp.max(s, axis=-1, keepdims=True))
        den = jnp.sum(p, axis=-1, keepdims=True)
        oh = _dot(p.astype(BF16), v_ref[0, 0, :, sl]) * (1.0 / den)
        outs.append(oh.astype(BF16))
    o = jnp.concatenate(outs, axis=-1)
    o_ref[0] = x + _dot(o, wo_ref[...])


def _cross_attention(x, norm_w, wq, k_all, v_all, wo, layer, *, tb=512):
    b, t, d = x.shape
    m = k_all.shape[2]
    return pl.pallas_call(
        functools.partial(_attn_body, heads=XATTN_HEADS),
        grid=(b, t // tb),
        in_specs=[
            pl.BlockSpec((1, tb, d), lambda i, j: (i, j, 0)),
            _resident((1, d)),
            _resident((d, d)),
            pl.BlockSpec((1, 1, m, d), lambda i, j: (layer, i, 0, 0)),
            pl.BlockSpec((1, 1, m, d), lambda i, j: (layer, i, 0, 0)),
            _resident((d, d)),
        ],
        out_specs=pl.BlockSpec((1, tb, d), lambda i, j: (i, j, 0)),
        out_shape=jax.ShapeDtypeStruct((b, t, d), F32),
        compiler_params=_params(2),
        name="memory_xattn",
    )(x, norm_w.reshape(1, d), wq.astype(BF16), k_all, v_all, wo.astype(BF16))


def _conf_body(x_ref, nw_ref, w1_ref, b1_ref, wdw_ref, bdw_ref, lnw_ref, lnb_ref, w2_ref, b2_ref,
               o_ref, u_ref, c_ref, *, taps, hist, t_step):
    tb, d = x_ref.shape[1], x_ref.shape[2]
    n_slab = d // LANES
    p_u = u_ref.shape[0] // n_slab
    p_c = c_ref.shape[0] // n_slab

    @pl.when(pl.program_id(1) == 0)
    def _():
        for s in range(n_slab):
            u_ref[s * p_u:s * p_u + hist, :] = jnp.zeros((hist, LANES), F32)

    x = x_ref[0]
    hb = (_rms(x) * nw_ref[...]).astype(BF16)
    ag = _dot(hb, w1_ref[...]) + b1_ref[...]
    u = ag[:, :d] * _sigmoid(ag[:, d:])
    for s in range(n_slab):
        u_ref[s * p_u + hist:s * p_u + hist + tb, :] = u[:, s * LANES:(s + 1) * LANES]

    w_taps = [wdw_ref[j] for j in range(taps)]
    bias = bdw_ref[...]

    def conv_step(i, carry):
        t0 = pl.multiple_of(i * t_step, t_step)
        accs = [bias] * t_step
        for k in range(t_step + taps - 1):
            uk = u_ref[pl.ds(t0 + hist - (taps - 1) + k, n_slab, stride=p_u), :]
            for tt in range(t_step):
                if 0 <= k - tt < taps:
                    accs[tt] = accs[tt] + w_taps[k - tt] * uk
        for tt in range(t_step):
            c_ref[pl.ds(t0 + tt, n_slab, stride=p_c), :] = accs[tt]
        return carry

    lax.fori_loop(0, tb // t_step, conv_step, 0)
    for s in range(n_slab):
        u_ref[s * p_u:s * p_u + hist, :] = u_ref[s * p_u + tb:s * p_u + tb + hist, :]

    c = jnp.concatenate([c_ref[s * p_c:s * p_c + tb, :] for s in range(n_slab)], axis=-1)
    mu = jnp.mean(c, axis=-1, keepdims=True)
    cc = c - mu
    var = jnp.mean(cc * cc, axis=-1, keepdims=True)
    yn = cc * lax.rsqrt(var + EPS) * lnw_ref[...] + lnb_ref[...]
    ys = _silu(yn).astype(BF16)
    o_ref[0] = x + _dot(ys, w2_ref[...]) + b2_ref[...]


def _conformer(x, norm_w, w_pw1, b_pw1, w_dw, b_dw, ln_w, ln_b, w_pw2, b_pw2, *, tb=512, t_step=8):
    b, t, d = x.shape
    taps = w_dw.shape[0]
    n_slab = d // LANES
    assert n_slab == SUBLANES
    hist = -(-(taps - 1) // SUBLANES) * SUBLANES
    p_u = hist + tb + 4
    p_c = tb + SUBLANES
    return pl.pallas_call(
        functools.partial(_conf_body, taps=taps, hist=hist, t_step=t_step),
        grid=(b, t // tb),
        in_specs=[
            pl.BlockSpec((1, tb, d), lambda i, j: (i, j, 0)),
            _resident((1, d)),
            _resident((d, 2 * d)),
            _resident((1, 2 * d)),
            _resident((taps, n_slab, LANES)),
            _resident((n_slab, LANES)),
            _resident((1, d)),
            _resident((1, d)),
            _resident((d, d)),
            _resident((1, d)),
        ],
        out_specs=pl.BlockSpec((1, tb, d), lambda i, j: (i, j, 0)),
        out_shape=jax.ShapeDtypeStruct((b, t, d), F32),
        scratch_shapes=[pltpu.VMEM((n_slab * p_u, LANES), F32), pltpu.VMEM((n_slab * p_c, LANES), F32)],
        compiler_params=_params(2),
        name="conformer_conv",
    )(x, norm_w.reshape(1, d), w_pw1.astype(BF16), b_pw1.reshape(1, 2 * d), w_dw.reshape(taps, n_slab, LANES),
      b_dw.reshape(n_slab, LANES), ln_w.reshape(1, d), ln_b.reshape(1, d), w_pw2.astype(BF16), b_pw2.reshape(1, d))


def _iota2(shape, axis):
    return lax.broadcasted_iota(jnp.int32, shape, axis)


HGRN_GROUP = SUB * SUBLANES
BLOCK_PITCH = SUB + 4
GROUP_PITCH = SUBLANES * BLOCK_PITCH
LOG2E = 1.4426950408889634


def _mix_body(x_ref, nw_ref, win_ref, lbl_ref, onw_ref, cw_ref, cb_ref, dtb_ref, alog_ref, dsk_ref,
              snw_ref, wout_ref, o_ref,
              pr_ref, qs_ref, ks_ref, bs_ref, vs_ref, od_ref, qd_ref, kd_ref, bd_ref, qx_ref, kx_ref,
              xbc_ref, xc_ref, xe_ref, ecs_ref, cs_ref, csr_ref, sth_ref, sts_ref, y_ref, *, e):
    tb, d = x_ref.shape[1], x_ref.shape[2]
    hk = HGRN_HEADS * HGRN_DK
    sd = SSD_HEADS * SSD_HEADDIM
    gn = SSD_GROUPS * SSD_STATE
    c_q, c_f, c_i, c_g = 0, hk, 2 * hk, 3 * hk
    c_z = 4 * hk
    c_x = c_z + sd
    c_dt = c_x + sd + 2 * gn
    hp = SSD_HEADS // SSD_GROUPS * SSD_HEADDIM
    taps = cw_ref.shape[0]
    P = PAD_ROWS
    n_groups = tb // HGRN_GROUP
    n_chunks = tb // CHUNK

    @pl.when(pl.program_id(1) == 0)
    def _():
        pr_ref[0:P, :] = jnp.zeros((P, pr_ref.shape[1]), F32)
        sth_ref[...] = jnp.zeros(sth_ref.shape, F32)
        sts_ref[...] = jnp.zeros(sts_ref.shape, F32)

    x = x_ref[0]
    hb = (_rms(x) * nw_ref[...]).astype(BF16)
    pr_ref[P:P + tb, :] = _dot(hb, win_ref[...])

    lg = lbl_ref[...]
    ex = jnp.exp(lg - jnp.max(lg, axis=0, keepdims=True))
    probs = ex / jnp.sum(ex, axis=0, keepdims=True)
    lb = jnp.sum(probs[0:e + 1], axis=0, keepdims=True) - probs[0:1]

    rg = _iota2((HGRN_GROUP, HGRN_GROUP), 0)
    cg = _iota2((HGRN_GROUP, HGRN_GROUP), 1)
    same_chunk = (rg // CHUNK) == (cg // CHUNK)
    m01 = lambda cond: jnp.where(same_chunk & cond, 1.0, 0.0).astype(BF16)
    tri_incl = m01(cg <= rg)
    r64 = _iota2((CHUNK, CHUNK), 0)
    c64 = _iota2((CHUNK, CHUNK), 1)
    tri64 = jnp.where(c64 <= r64, 1.0, 0.0).astype(BF16)
    lane2 = _iota2((CHUNK, LANES), 1)
    causal2 = (lane2 % CHUNK) <= _iota2((CHUNK, LANES), 0)
    rb = _iota2((tb, tb), 0)
    cb_ = _iota2((tb, tb), 1)
    tri_rows = jnp.where(((rb // CHUNK) == (cb_ // CHUNK)) & (rb <= cb_), 1.0, 0.0).astype(BF16)
    n_pair = SSD_HEADS // 2
    er = _iota2((LANES, sd), 0)
    ehead = jnp.where(er < n_pair, 2 * er, 2 * (er - n_pair) + 1)
    expand = jnp.where((_iota2((LANES, sd), 1) // SSD_HEADDIM == ehead) & (er < SSD_HEADS), 1.0, 0.0).astype(BF16)
    ones_k = jnp.ones((HGRN_DK, HGRN_DK), BF16)
    pair_diag = (_iota2((LANES, LANES), 0) // SSD_HEADDIM) == (_iota2((LANES, LANES), 1) // SSD_HEADDIM)

    for g in range(n_groups):
        rows = slice(P + g * HGRN_GROUP, P + (g + 1) * HGRN_GROUP)
        orow = slice(g * HGRN_GROUP, (g + 1) * HGRN_GROUP)
        fr = pr_ref[rows, c_f:c_f + hk]
        lf = jnp.log(lb + (1.0 - lb) * _sigmoid(fr)) * LOG2E
        kk = (1.0 - lb) * _sigmoid(-fr)
        bcs = _dot01(tri_incl, lf)
        nb = CHUNK // SUB
        n_blk = HGRN_GROUP // SUB
        ends = [bcs[(i + 1) * SUB - 1:(i + 1) * SUB, :] for i in range(n_blk)]
        zero_row = jnp.zeros((1, hk), F32)

        def at_block_end(back):
            rows_ = [ends[i - back] if (i % nb) - back >= 0 else zero_row for i in range(n_blk)]
            return jnp.concatenate([jnp.broadcast_to(r_, (SUB, hk)) for r_ in rows_], axis=0)

        btot = jnp.concatenate([jnp.broadcast_to(ends[(i // nb) * nb + nb - 1], (SUB, hk)) for i in range(n_blk)], axis=0)
        qq = pr_ref[rows, c_q:c_q + hk] * (HGRN_DK ** -0.5)
        vv = pr_ref[rows, c_i:c_i + hk]
        qd_ref[orow, :] = (qq * jnp.exp2(bcs)).astype(BF16)
        kd_ref[orow, :] = (kk * jnp.exp2(btot - bcs)).astype(BF16)
        bd_ref[orow, :] = jnp.exp2(btot)
        kx_ref[orow, :] = (kk * jnp.exp2(at_block_end(0) - bcs)).astype(BF16)
        for m in range(1, nb):
            qx_ref[m - 1, orow, :] = (qq * jnp.exp2(bcs - at_block_end(m))).astype(BF16)
        for h in range(HGRN_HEADS):
            hs = slice(h * HGRN_DK, (h + 1) * HGRN_DK)
            for blk in range(SUBLANES):
                src = slice(blk * SUB, (blk + 1) * SUB)
                lo = (h * n_groups + g) * GROUP_PITCH + blk * BLOCK_PITCH
                qs_ref[lo:lo + SUB, :] = qq[src, hs]
                ks_ref[lo:lo + SUB, :] = kk[src, hs]
                bs_ref[lo:lo + SUB, :] = bcs[src, hs]
                vs_ref[lo:lo + SUB, :] = vv[src, hs]

    def pair_step(idx, carry):
        sb = pl.multiple_of(idx * GROUP_PITCH, SUBLANES)
        ob = pl.multiple_of(idx * HGRN_GROUP, HGRN_GROUP)
        ld = lambda ref, r: ref[pl.ds(sb + r, SUBLANES, stride=BLOCK_PITCH), :]
        qv = [ld(qs_ref, r) for r in range(SUB)]
        kv = [ld(ks_ref, r) for r in range(SUB)]
        bv = [ld(bs_ref, r) for r in range(SUB)]
        vv_ = [ld(vs_ref, r) for r in range(SUB)]
        prods = []
        for r in range(SUB):
            for s_ in range(r + 1):
                if s_ == r:
                    prods.append(qv[r] * kv[r])
                else:
                    prods.append(qv[r] * kv[s_] * jnp.exp2(bv[r] - bv[s_]))
        sc = _dot(jnp.concatenate(prods, axis=0).astype(BF16), ones_k)
        n = 0
        for r in range(SUB):
            acc = None
            for s_ in range(r + 1):
                term = sc[n * SUBLANES:(n + 1) * SUBLANES, :] * vv_[s_]
                acc = term if acc is None else acc + term
                n += 1
            od_ref[pl.ds(ob + r, SUBLANES, stride=SUB), :] = acc
        return carry

    lax.fori_loop(0, HGRN_HEADS * n_groups, pair_step, 0, unroll=2)

    conv = jnp.broadcast_to(cb_ref[...], (tb, cw_ref.shape[1]))
    for j in range(taps):
        lo = P - (taps - 1) + j
        conv = conv + cw_ref[j:j + 1, :] * pr_ref[lo:lo + tb, c_x:c_x + sd + 2 * gn]
    xbc_ref[...] = _silu(conv)
    pr_ref[0:P, c_x:c_x + sd + 2 * gn] = pr_ref[tb:tb + P, c_x:c_x + sd + 2 * gn]
    dt = jnp.logaddexp(pr_ref[P:P + tb, c_dt:c_dt + LANES] + dtb_ref[...], 0.0)
    da = dt * (-jnp.exp(alog_ref[...]))
    d0, d1, d2 = _split3(da)
    for c in range(n_chunks):
        cr = slice(c * CHUNK, (c + 1) * CHUNK)
        cs_ref[cr, :] = _dot(tri64, d0[cr]) + _dot(tri64, d1[cr]) + _dot(tri64, d2[cr])
    cs = cs_ref[...]
    cs_end = jnp.concatenate(
        [jnp.broadcast_to(cs[(c + 1) * CHUNK - 1:(c + 1) * CHUNK, :], (CHUNK, LANES)) for c in range(n_chunks)], axis=0)
    cs_t = _dot_tn(d0, tri_rows) + _dot_tn(d1, tri_rows) + _dot_tn(d2, tri_rows)
    for c in range(n_chunks):
        cl = slice(c * CHUNK, (c + 1) * CHUNK)
        csr_ref[c] = jnp.concatenate([cs_t[0:SUBLANES, cl], cs_t[n_pair:n_pair + SUBLANES, cl]], axis=1)
    e0, e1, e2 = _split3(jnp.exp(cs))
    spread = _dot(jnp.concatenate([dt.astype(BF16), jnp.exp(cs_end - cs).astype(BF16), e0, e1, e2], axis=0), expand)
    xc = xbc_ref[:, 0:sd] * spread[0:tb]
    xc_ref[...] = xc.astype(BF16)
    xe_ref[...] = (xc * spread[tb:2 * tb]).astype(BF16)
    ecs_ref[...] = spread[2 * tb:3 * tb] + spread[3 * tb:4 * tb] + spread[4 * tb:5 * tb]

    def chunk_step(c, carry):
        r0 = pl.multiple_of(c * CHUNK, CHUNK)
        rows = pl.ds(r0, CHUNK)
        prow = pl.ds(r0 + P, CHUNK)

        gate = _silu(pr_ref[prow, c_g:c_g + hk])
        nb = CHUNK // SUB
        lane64 = _iota2((CHUNK, CHUNK), 1)
        for h in range(HGRN_HEADS):
            hs = slice(h * HGRN_DK, (h + 1) * HGRN_DK)
            qd = qd_ref[rows, hs]
            kd = kd_ref[rows, hs]
            vi = pr_ref[prow, c_i + h * HGRN_DK:c_i + (h + 1) * HGRN_DK].astype(BF16)
            lhs = jnp.concatenate(
                [qx_ref[i - j - 1, pl.ds(r0 + i * SUB, SUB), hs] for j in range(nb - 1) for i in range(j + 1, nb)], axis=0)
            res = _dot_nt(lhs, kx_ref[rows, hs])
            sc = jnp.zeros((CHUNK, CHUNK), F32)
            off = 0
            for j in range(nb - 1):
                n_r = (nb - 1 - j) * SUB
                col = jnp.concatenate([jnp.zeros((CHUNK - n_r, CHUNK), F32), res[off:off + n_r, :]], axis=0)
                sc = jnp.where((lane64 // SUB) == j, col, sc)
                off += n_r
            st = sth_ref[h]
            o_h = (od_ref[pl.ds(h * tb + r0, CHUNK), :] + _dot(sc.astype(BF16), vi)
                   + _dot_nt(qd, st.astype(BF16)))
            sth_ref[h] = st * bd_ref[pl.ds(r0, 1), hs] + _dot_tn(vi, kd)
            o_h = _rms(o_h) * onw_ref[:, hs] * gate[:, hs]
            y_ref[rows, hs] = o_h.astype(BF16)

        xbc = xbc_ref[rows, :]
        xs = xbc[:, 0:sd]
        cs_c = cs_ref[rows, :]
        ecs = ecs_ref[rows, :]
        xcb = xc_ref[rows, :]
        xcd = xe_ref[rows, :]
        cs_rows = csr_ref[c]
        ys = []
        for g in range(SSD_GROUPS):
            bm = xbc[:, sd + g * SSD_STATE:sd + (g + 1) * SSD_STATE].astype(BF16)
            cm = xbc[:, sd + gn + g * SSD_STATE:sd + gn + (g + 1) * SSD_STATE].astype(BF16)
            cbm2 = _dot_nt(cm, jnp.concatenate([bm, bm], axis=0))
            gl = slice(g * hp, (g + 1) * hp)
            st = sts_ref[g]
            y_off = _dot(cm, st.astype(BF16)) * ecs[:, gl]
            sts_ref[g] = st * ecs[CHUNK - 1:CHUNK, gl] + _dot_tn(bm, xcd[:, gl])
            for j in range(SSD_HEADS // SSD_GROUPS // 2):
                p = g * (SSD_HEADS // SSD_GROUPS // 2) + j
                cs_pair = jnp.where(lane2 < CHUNK, cs_c[:, p:p + 1], cs_c[:, n_pair + p:n_pair + p + 1])
                dec = jnp.where(causal2, jnp.exp(cs_pair - cs_rows[p:p + 1, :]), 0.0)
                xp = xcb[:, p * LANES:(p + 1) * LANES]
                rhs = jnp.where(pair_diag, jnp.concatenate([xp, xp], axis=0), jnp.zeros((LANES, LANES), BF16))
                yd = _dot((cbm2 * dec).astype(BF16), rhs)
                ys.append(yd + y_off[:, j * LANES:(j + 1) * LANES])
        y = jnp.concatenate(ys, axis=-1) + xs * dsk_ref[...]
        y = y * _silu(pr_ref[prow, c_z:c_z + sd])
        y = jnp.concatenate([_rms(y[:, g * hp:(g + 1) * hp]) for g in range(SSD_GROUPS)], axis=-1) * snw_ref[...]
        y_ref[rows, hk:hk + sd] = y.astype(BF16)
        return carry

    lax.fori_loop(0, n_chunks, chunk_step, 0, unroll=4)

    o_ref[0] = x + _dot(y_ref[...], wout_ref[...])


def _mixer(x, norm_w, w_in, lb_logits, out_norm_w, conv_w, conv_b, dt_bias, a_log, d_skip, ssd_norm_w,
           w_out, e, *, tb=512):
    b, t, d = x.shape
    n_even, hk = lb_logits.shape
    sd = SSD_HEADS * SSD_HEADDIM
    ab_in = w_in.shape[1]
    pw = -(-ab_in // LANES) * LANES
    n_dt = SSD_HEADS
    order = list(range(0, n_dt, 2)) + list(range(1, n_dt, 2))
    w_dt = w_in[:, ab_in - n_dt:][:, order]
    w_in_p = jnp.pad(jnp.concatenate([w_in[:, :ab_in - n_dt], w_dt], axis=1), ((0, 0), (0, pw - ab_in))).astype(BF16)
    pad8 = lambda a: jnp.pad(a[jnp.array(order)].reshape(1, -1), ((0, 0), (0, LANES - a.shape[-1])))
    cc = conv_w.shape[1]
    assert tb % HGRN_GROUP == 0
    slab_rows = HGRN_HEADS * (tb // HGRN_GROUP) * GROUP_PITCH
    slab = pltpu.VMEM((slab_rows, LANES), F32)
    return pl.pallas_call(
        functools.partial(_mix_body, e=e),
        grid=(b, t // tb),
        in_specs=[
            pl.BlockSpec((1, tb, d), lambda i, j: (i, j, 0)),
            _resident((1, d)),
            _resident((d, pw)),
            _resident((n_even, hk)),
            _resident((1, hk)),
            _resident((conv_w.shape[0], cc)),
            _resident((1, cc)),
            _resident((1, LANES)),
            _resident((1, LANES)),
            _resident((1, sd)),
            _resident((1, sd)),
            _resident((hk + sd, d)),
        ],
        out_specs=pl.BlockSpec((1, tb, d), lambda i, j: (i, j, 0)),
        out_shape=jax.ShapeDtypeStruct((b, t, d), F32),
        scratch_shapes=[
            pltpu.VMEM((PAD_ROWS + tb, pw), F32),
            slab, slab, slab, slab,
            pltpu.VMEM((HGRN_HEADS * tb, LANES), F32),
            pltpu.VMEM((tb, hk), BF16),
            pltpu.VMEM((tb, hk), BF16),
            pltpu.VMEM((tb, hk), F32),
            pltpu.VMEM((CHUNK // SUB - 1, tb, hk), BF16),
            pltpu.VMEM((tb, hk), BF16),
            pltpu.VMEM((tb, cc), F32),
            pltpu.VMEM((tb, sd), BF16),
            pltpu.VMEM((tb, sd), BF16),
            pltpu.VMEM((tb, sd), F32),
            pltpu.VMEM((tb, LANES), F32),
            pltpu.VMEM((tb // CHUNK, SUBLANES, LANES), F32),
            pltpu.VMEM((HGRN_HEADS, HGRN_DK, HGRN_DK), F32),
            pltpu.VMEM((SSD_GROUPS, SSD_STATE, sd // SSD_GROUPS), F32),
            pltpu.VMEM((tb, hk + sd), BF16),
        ],
        compiler_params=_params(2),
        name="hgrn_ssd_mixer",
    )(x, norm_w.reshape(1, d), w_in_p, lb_logits, out_norm_w.reshape(1, hk), conv_w, conv_b.reshape(1, cc),
      pad8(dt_bias), pad8(a_log), jnp.repeat(d_skip, SSD_HEADDIM).reshape(1, sd), ssd_norm_w.reshape(1, sd),
      w_out.astype(BF16))


def kernel(x, mem, mem_norm_w, norm_mix_w, ab_w_in, hgrn_lb_logits, hgrn_out_norm_w, ssd_conv_w, ssd_conv_b,
           ssd_dt_bias, ssd_a_log, ssd_d, ssd_norm_w, ab_w_out, cv_w_pw1, cv_b_pw1, cv_w_dw, cv_b_dw, cv_ln_w,
           cv_ln_b, cv_w_pw2, cv_b_pw2, norm_xattn_w, xattn_wq, xattn_wk, xattn_wv, xattn_wo, norm_mlp_w,
           mlp_w1, mlp_w2, final_norm_w):
    b, t, d = x.shape
    depth = norm_mix_w.shape[0]
    k_all, v_all = _memory_kv(mem, mem_norm_w, xattn_wk, xattn_wv)
    for layer in range(depth):
        if layer % 2 == 0:
            e = layer // 2
            x = _mixer(x, norm_mix_w[layer], ab_w_in[e], hgrn_lb_logits, hgrn_out_norm_w[e], ssd_conv_w[e],
                       ssd_conv_b[e], ssd_dt_bias[e], ssd_a_log[e], ssd_d[e], ssd_norm_w[e], ab_w_out[e], e)
        else:
            o = layer // 2
            x = _conformer(x, norm_mix_w[layer], cv_w_pw1[o], cv_b_pw1[o], cv_w_dw[o], cv_b_dw[o], cv_ln_w[o],
                           cv_ln_b[o], cv_w_pw2[o], cv_b_pw2[o])
        x = _cross_attention(x, norm_xattn_w[layer], xattn_wq[layer], k_all, v_all, xattn_wo[layer], layer)
        x = _mlp(x.reshape(b * t, d), norm_mlp_w[layer], mlp_w1[layer], mlp_w2[layer], final_norm_w,
                 final=(layer == depth - 1)).reshape(b, t, d)
    return x
```

```python
import functools

import jax
import jax.numpy as jnp
from jax import lax
from jax.experimental import pallas as pl
from jax.experimental.pallas import tpu as pltpu

F32 = jnp.float32
BF16 = jnp.bfloat16

EPS = 1e-6
CHUNK = 64
SUB = 16
LANES = 128
SUBLANES = 8
PAD_ROWS = 8

HGRN_HEADS = 4
HGRN_DK = 128
SSD_HEADS = 8
SSD_HEADDIM = 64
SSD_GROUPS = 2
SSD_STATE = 128
XATTN_HEADS = 4

VMEM_LIMIT_BYTES = 56 * 1024 * 1024


def _rms(x):
    return x * lax.rsqrt(jnp.mean(x * x, axis=-1, keepdims=True) + EPS)


def _sigmoid(x):
    return 1.0 / (1.0 + jnp.exp(-x))


def _silu(x):
    return x * _sigmoid(x)


def _dot(a, b):
    return jnp.dot(a, b, preferred_element_type=F32)


def _dot_nt(a, b):
    return lax.dot_general(a, b, (((1,), (1,)), ((), ())), preferred_element_type=F32)


def _dot_tn(a, b):
    return lax.dot_general(a, b, (((0,), (0,)), ((), ())), preferred_element_type=F32)


def _split3(a):
    a0 = a.astype(BF16)
    r1 = a - a0.astype(F32)
    a1 = r1.astype(BF16)
    a2 = (r1 - a1.astype(F32)).astype(BF16)
    return a0, a1, a2


def _dot01(m01, a):
    a0, a1, a2 = _split3(a)
    return _dot(m01, a0) + _dot(m01, a1) + _dot(m01, a2)


def _resident(shape):
    nd = len(shape)
    return pl.BlockSpec(shape, lambda *_: (0,) * nd, pipeline_mode=pl.Buffered(1))


def _params(n_axes):
    return pltpu.CompilerParams(dimension_semantics=("arbitrary",) * n_axes,
                                vmem_limit_bytes=VMEM_LIMIT_BYTES)


def _kv_body(mem_ref, mw_ref, wk_ref, wv_ref, k_ref, v_ref):
    m = mem_ref[0]
    mn = (_rms(m) * mw_ref[...]).astype(BF16)
    k_ref[0, 0] = _dot(mn, wk_ref[0].astype(BF16)).astype(BF16)
    v_ref[0, 0] = _dot(mn, wv_ref[0].astype(BF16)).astype(BF16)


def _memory_kv(mem, mem_norm_w, wk, wv):
    depth, d, _ = wk.shape
    b, m, _ = mem.shape
    out = jax.ShapeDtypeStruct((depth, b, m, d), BF16)
    return pl.pallas_call(
        _kv_body,
        grid=(depth, b),
        in_specs=[
            pl.BlockSpec((1, m, d), lambda l, i: (i, 0, 0)),
            pl.BlockSpec((1, d), lambda l, i: (0, 0)),
            pl.BlockSpec((1, d, d), lambda l, i: (l, 0, 0)),
            pl.BlockSpec((1, d, d), lambda l, i: (l, 0, 0)),
        ],
        out_specs=[
            pl.BlockSpec((1, 1, m, d), lambda l, i: (l, i, 0, 0)),
            pl.BlockSpec((1, 1, m, d), lambda l, i: (l, i, 0, 0)),
        ],
        out_shape=[out, out],
        compiler_params=_params(2),
        name="memory_kv",
    )(mem, mem_norm_w.reshape(1, d), wk, wv)


def _mlp_body(x_ref, nw_ref, w1_ref, w2_ref, fw_ref, o_ref, *, ff_chunk, final):
    x = x_ref[...]
    hb = (_rms(x) * nw_ref[...]).astype(BF16)
    acc = x
    for c in range(w1_ref.shape[1] // ff_chunk):
        a = _dot(hb, w1_ref[:, c * ff_chunk:(c + 1) * ff_chunk])
        a = jnp.square(jnp.maximum(a, 0.0)).astype(BF16)
        acc = acc + _dot(a, w2_ref[c * ff_chunk:(c + 1) * ff_chunk, :])
    if final:
        acc = _rms(acc) * fw_ref[...]
    o_ref[...] = acc


def _mlp(x2, norm_w, w1, w2, final_w, *, final, tb=1024, ff_chunk=1024):
    n, d = x2.shape
    dff = w1.shape[1]
    return pl.pallas_call(
        functools.partial(_mlp_body, ff_chunk=ff_chunk, final=final),
        grid=(n // tb,),
        in_specs=[
            pl.BlockSpec((tb, d), lambda i: (i, 0)),
            _resident((1, d)),
            _resident((d, dff)),
            _resident((dff, d)),
            _resident((1, d)),
        ],
        out_specs=pl.BlockSpec((tb, d), lambda i: (i, 0)),
        out_shape=jax.ShapeDtypeStruct((n, d), F32),
        compiler_params=_params(1),
        name="sq_relu_mlp",
    )(x2, norm_w.reshape(1, d), w1.astype(BF16), w2.astype(BF16), final_w.reshape(1, d))


def _attn_body(x_ref, nw_ref, wq_ref, k_ref, v_ref, wo_ref, o_ref, *, heads):
    x = x_ref[0]
    d = x.shape[-1]
    hd = d // heads
    hb = (_rms(x) * nw_ref[...]).astype(BF16)
    q = (_dot(hb, wq_ref[...]) * (hd ** -0.5)).astype(BF16)
    outs = []
    for h in range(heads):
        sl = slice(h * hd, (h + 1) * hd)
        s = _dot_nt(q[:, sl], k_ref[0, 0, :, sl])
        p = jnp.exp(s - jnp.max(s, axis=-1, keepdims=True))
        den = jnp.sum(p, axis=-1, keepdims=True)
        oh = _dot(p.astype(BF16), v_ref[0, 0, :, sl]) * (1.0 / den)
        outs.append(oh.astype(BF16))
    o = jnp.concatenate(outs, axis=-1)
    o_ref[0] = x + _dot(o, wo_ref[...])


def _cross_attention(x, norm_w, wq, k_all, v_all, wo, layer, *, tb=1024):
    b, t, d = x.shape
    m = k_all.shape[2]
    return pl.pallas_call(
        functools.partial(_attn_body, heads=XATTN_HEADS),
        grid=(b, t // tb),
        in_specs=[
            pl.BlockSpec((1, tb, d), lambda i, j: (i, j, 0)),
            _resident((1, d)),
            _resident((d, d)),
            pl.BlockSpec((1, 1, m, d), lambda i, j: (layer, i, 0, 0)),
            pl.BlockSpec((1, 1, m, d), lambda i, j: (layer, i, 0, 0)),
            _resident((d, d)),
        ],
        out_specs=pl.BlockSpec((1, tb, d), lambda i, j: (i, j, 0)),
        out_shape=jax.ShapeDtypeStruct((b, t, d), F32),
        compiler_params=_params(2),
        name="memory_xattn",
    )(x, norm_w.reshape(1, d), wq.astype(BF16), k_all, v_all, wo.astype(BF16))


def _slab_conv(u_ref, c_ref, w_taps, bias, *, hist, n_t, t_step):
    taps = len(w_taps)
    n_slab = bias.shape[0]
    p_u = u_ref.shape[0] // n_slab
    p_c = c_ref.shape[0] // n_slab

    def step(i, carry):
        t0 = pl.multiple_of(i * t_step, t_step)
        accs = [bias] * t_step
        for k in range(t_step + taps - 1):
            uk = u_ref[pl.ds(t0 + hist - (taps - 1) + k, n_slab, stride=p_u), :]
            for tt in range(t_step):
                if 0 <= k - tt < taps:
                    accs[tt] = accs[tt] + w_taps[k - tt] * uk
        for tt in range(t_step):
            c_ref[pl.ds(t0 + tt, n_slab, stride=p_c), :] = accs[tt]
        return carry

    lax.fori_loop(0, n_t // t_step, step, 0)


def _slab_pitches(hist, n_t):
    return hist + n_t + 4, n_t + SUBLANES


def _conf_body(x_ref, nw_ref, w1_ref, b1_ref, wdw_ref, bdw_ref, lnw_ref, lnb_ref, w2_ref, b2_ref,
               o_ref, u_ref, c_ref, *, taps, hist, t_step):
    tb, d = x_ref.shape[1], x_ref.shape[2]
    n_slab = d // LANES
    p_u = u_ref.shape[0] // n_slab
    p_c = c_ref.shape[0] // n_slab

    @pl.when(pl.program_id(1) == 0)
    def _():
        for s in range(n_slab):
            u_ref[s * p_u:s * p_u + hist, :] = jnp.zeros((hist, LANES), F32)

    x = x_ref[0]
    hb = (_rms(x) * nw_ref[...]).astype(BF16)
    ag = _dot(hb, w1_ref[...]) + b1_ref[...]
    u = ag[:, :d] * _sigmoid(ag[:, d:])
    for s in range(n_slab):
        u_ref[s * p_u + hist:s * p_u + hist + tb, :] = u[:, s * LANES:(s + 1) * LANES]

    _slab_conv(u_ref, c_ref, [wdw_ref[j] for j in range(taps)], bdw_ref[...], hist=hist, n_t=tb, t_step=t_step)
    for s in range(n_slab):
        u_ref[s * p_u:s * p_u + hist, :] = u_ref[s * p_u + tb:s * p_u + tb + hist, :]

    c = jnp.concatenate([c_ref[s * p_c:s * p_c + tb, :] for s in range(n_slab)], axis=-1)
    mu = jnp.mean(c, axis=-1, keepdims=True)
    cc = c - mu
    var = jnp.mean(cc * cc, axis=-1, keepdims=True)
    yn = cc * lax.rsqrt(var + EPS) * lnw_ref[...] + lnb_ref[...]
    ys = _silu(yn).astype(BF16)
    o_ref[0] = x + _dot(ys, w2_ref[...]) + b2_ref[...]


def _conformer(x, norm_w, w_pw1, b_pw1, w_dw, b_dw, ln_w, ln_b, w_pw2, b_pw2, *, tb=512, t_step=16):
    b, t, d = x.shape
    taps = w_dw.shape[0]
    n_slab = d // LANES
    assert n_slab == SUBLANES
    hist = -(-(taps - 1) // SUBLANES) * SUBLANES
    p_u, p_c = _slab_pitches(hist, tb)
    return pl.pallas_call(
        functools.partial(_conf_body, taps=taps, hist=hist, t_step=t_step),
        grid=(b, t // tb),
        in_specs=[
            pl.BlockSpec((1, tb, d), lambda i, j: (i, j, 0)),
            _resident((1, d)),
            _resident((d, 2 * d)),
            _resident((1, 2 * d)),
            _resident((taps, n_slab, LANES)),
            _resident((n_slab, LANES)),
            _resident((1, d)),
            _resident((1, d)),
            _resident((d, d)),
            _resident((1, d)),
        ],
        out_specs=pl.BlockSpec((1, tb, d), lambda i, j: (i, j, 0)),
        out_shape=jax.ShapeDtypeStruct((b, t, d), F32),
        scratch_shapes=[pltpu.VMEM((n_slab * p_u, LANES), F32), pltpu.VMEM((n_slab * p_c, LANES), F32)],
        compiler_params=_params(2),
        name="conformer_conv",
    )(x, norm_w.reshape(1, d), w_pw1.astype(BF16), b_pw1.reshape(1, 2 * d), w_dw.reshape(taps, n_slab, LANES),
      b_dw.reshape(n_slab, LANES), ln_w.reshape(1, d), ln_b.reshape(1, d), w_pw2.astype(BF16), b_pw2.reshape(1, d))


def _iota2(shape, axis):
    return lax.broadcasted_iota(jnp.int32, shape, axis)


HGRN_GROUP = SUB * SUBLANES
BLOCK_PITCH = SUB + 4
GROUP_PITCH = SUBLANES * BLOCK_PITCH
LOG2E = 1.4426950408889634


def _mix_body(x_ref, nw_ref, win_ref, lbl_ref, onw_ref, cw_ref, cb_ref, dtb_ref, alog_ref, dsk_ref,
              snw_ref, wout_ref, o_ref,
              pr_ref, qs_ref, ks_ref, bs_ref, vs_ref, od_ref, qd_ref, kd_ref, bd_ref, qx_ref, kx_ref,
              su_ref, sc_ref, xbc_ref, xc_ref, xe_ref, ecs_ref, cs_ref, csr_ref, sth_ref, sts_ref, y_ref, *, e, t_step):
    tb, d = x_ref.shape[1], x_ref.shape[2]
    hk = HGRN_HEADS * HGRN_DK
    sd = SSD_HEADS * SSD_HEADDIM
    gn = SSD_GROUPS * SSD_STATE
    c_q, c_f, c_i, c_g = 0, hk, 2 * hk, 3 * hk
    c_z = 4 * hk
    c_x = c_z + sd
    c_dt = c_x + sd + 2 * gn
    hp = SSD_HEADS // SSD_GROUPS * SSD_HEADDIM
    taps = cw_ref.shape[0]
    hist = PAD_ROWS
    n_slab = cb_ref.shape[0]
    p_u = su_ref.shape[0] // n_slab
    p_c = sc_ref.shape[0] // n_slab
    n_groups = tb // HGRN_GROUP
    n_chunks = tb // CHUNK

    @pl.when(pl.program_id(1) == 0)
    def _():
        for s in range(n_slab):
            su_ref[s * p_u:s * p_u + hist, :] = jnp.zeros((hist, LANES), F32)
        sth_ref[...] = jnp.zeros(sth_ref.shape, F32)
        sts_ref[...] = jnp.zeros(sts_ref.shape, F32)

    x = x_ref[0]
    hb = (_rms(x) * nw_ref[...]).astype(BF16)
    pr_ref[:, 0:c_g] = _dot(hb, win_ref[:, 0:c_g])

    lg = lbl_ref[...]
    ex = jnp.exp(lg - jnp.max(lg, axis=0, keepdims=True))
    probs = ex / jnp.sum(ex, axis=0, keepdims=True)
    lb = jnp.sum(probs[0:e + 1], axis=0, keepdims=True) - probs[0:1]

    rg = _iota2((HGRN_GROUP, HGRN_GROUP), 0)
    cg = _iota2((HGRN_GROUP, HGRN_GROUP), 1)
    same_chunk = (rg // CHUNK) == (cg // CHUNK)
    m01 = lambda cond: jnp.where(same_chunk & cond, 1.0, 0.0).astype(BF16)
    tri_incl = m01(cg <= rg)
    r64 = _iota2((CHUNK, CHUNK), 0)
    c64 = _iota2((CHUNK, CHUNK), 1)
    tri64 = jnp.where(c64 <= r64, 1.0, 0.0).astype(BF16)
    lane2 = _iota2((CHUNK, LANES), 1)
    causal2 = (lane2 % CHUNK) <= _iota2((CHUNK, LANES), 0)
    rb = _iota2((tb, tb), 0)
    cb_ = _iota2((tb, tb), 1)
    tri_rows = jnp.where(((rb // CHUNK) == (cb_ // CHUNK)) & (rb <= cb_), 1.0, 0.0).astype(BF16)
    n_pair = SSD_HEADS // 2
    er = _iota2((LANES, sd), 0)
    ehead = jnp.where(er < n_pair, 2 * er, 2 * (er - n_pair) + 1)
    expand = jnp.where((_iota2((LANES, sd), 1) // SSD_HEADDIM == ehead) & (er < SSD_HEADS), 1.0, 0.0).astype(BF16)
    ones_k = jnp.ones((HGRN_DK, HGRN_DK), BF16)
    pair_diag = (_iota2((LANES, LANES), 0) // SSD_HEADDIM) == (_iota2((LANES, LANES), 1) // SSD_HEADDIM)

    for g in range(n_groups):
        rows = slice(g * HGRN_GROUP, (g + 1) * HGRN_GROUP)
        orow = rows
        fr = pr_ref[rows, c_f:c_f + hk]
        lf = jnp.log(lb + (1.0 - lb) * _sigmoid(fr)) * LOG2E
        kk = (1.0 - lb) * _sigmoid(-fr)
        bcs = _dot01(tri_incl, lf)
        nb = CHUNK // SUB
        n_blk = HGRN_GROUP // SUB
        ends = [bcs[(i + 1) * SUB - 1:(i + 1) * SUB, :] for i in range(n_blk)]
        zero_row = jnp.zeros((1, hk), F32)

        def at_block_end(back):
            rows_ = [ends[i - back] if (i % nb) - back >= 0 else zero_row for i in range(n_blk)]
            return jnp.concatenate([jnp.broadcast_to(r_, (SUB, hk)) for r_ in rows_], axis=0)

        btot = jnp.concatenate([jnp.broadcast_to(ends[(i // nb) * nb + nb - 1], (SUB, hk)) for i in range(n_blk)], axis=0)
        qq = pr_ref[rows, c_q:c_q + hk] * (HGRN_DK ** -0.5)
        vv = pr_ref[rows, c_i:c_i + hk]
        qd_ref[orow, :] = (qq * jnp.exp2(bcs)).astype(BF16)
        kd_ref[orow, :] = (kk * jnp.exp2(btot - bcs)).astype(BF16)
        bd_ref[orow, :] = jnp.exp2(btot)
        kx_ref[orow, :] = (kk * jnp.exp2(at_block_end(0) - bcs)).astype(BF16)
        for m in range(1, nb):
            qx_ref[m - 1, orow, :] = (qq * jnp.exp2(bcs - at_block_end(m))).astype(BF16)
        for h in range(HGRN_HEADS):
            hs = slice(h * HGRN_DK, (h + 1) * HGRN_DK)
            for blk in range(SUBLANES):
                src = slice(blk * SUB, (blk + 1) * SUB)
                lo = (h * n_groups + g) * GROUP_PITCH + blk * BLOCK_PITCH
                qs_ref[lo:lo + SUB, :] = qq[src, hs]
                ks_ref[lo:lo + SUB, :] = kk[src, hs]
                bs_ref[lo:lo + SUB, :] = bcs[src, hs]
                vs_ref[lo:lo + SUB, :] = vv[src, hs]

    pr_ref[:, c_g:] = _dot(hb, win_ref[:, c_g:])

    def pair_step(idx, carry):
        sb = pl.multiple_of(idx * GROUP_PITCH, SUBLANES)
        ob = pl.multiple_of(idx * HGRN_GROUP, HGRN_GROUP)
        ld = lambda ref, r: ref[pl.ds(sb + r, SUBLANES, stride=BLOCK_PITCH), :]
        qv = [ld(qs_ref, r) for r in range(SUB)]
        kv = [ld(ks_ref, r) for r in range(SUB)]
        bv = [ld(bs_ref, r) for r in range(SUB)]
        vv_ = [ld(vs_ref, r) for r in range(SUB)]
        prods = []
        for r in range(SUB):
            for s_ in range(r + 1):
                if s_ == r:
                    prods.append(qv[r] * kv[r])
                else:
                    prods.append(qv[r] * kv[s_] * jnp.exp2(bv[r] - bv[s_]))
        sc = _dot(jnp.concatenate(prods, axis=0).astype(BF16), ones_k)
        n = 0
        for r in range(SUB):
            acc = None
            for s_ in range(r + 1):
                term = sc[n * SUBLANES:(n + 1) * SUBLANES, :] * vv_[s_]
                acc = term if acc is None else acc + term
                n += 1
            od_ref[pl.ds(ob + r, SUBLANES, stride=SUB), :] = acc
        return carry

    lax.fori_loop(0, HGRN_HEADS * n_groups, pair_step, 0, unroll=2)

    for s in range(n_slab):
        su_ref[s * p_u + hist:s * p_u + hist + tb, :] = pr_ref[:, c_x + s * LANES:c_x + (s + 1) * LANES]
    _slab_conv(su_ref, sc_ref, [cw_ref[j] for j in range(taps)], cb_ref[...], hist=hist, n_t=tb, t_step=t_step)
    for s in range(n_slab):
        su_ref[s * p_u:s * p_u + hist, :] = su_ref[s * p_u + tb:s * p_u + tb + hist, :]
    xbc_ref[...] = _silu(jnp.concatenate([sc_ref[s * p_c:s * p_c + tb, :] for s in range(n_slab)], axis=-1))
    dt = jnp.logaddexp(pr_ref[:, c_dt:c_dt + LANES] + dtb_ref[...], 0.0)
    da = dt * (-jnp.exp(alog_ref[...]))
    d0, d1, d2 = _split3(da)
    for c in range(n_chunks):
        cr = slice(c * CHUNK, (c + 1) * CHUNK)
        cs_ref[cr, :] = _dot(tri64, d0[cr]) + _dot(tri64, d1[cr]) + _dot(tri64, d2[cr])
    cs = cs_ref[...]
    cs_end = jnp.concatenate(
        [jnp.broadcast_to(cs[(c + 1) * CHUNK - 1:(c + 1) * CHUNK, :], (CHUNK, LANES)) for c in range(n_chunks)], axis=0)
    cs_t = _dot_tn(d0, tri_rows) + _dot_tn(d1, tri_rows) + _dot_tn(d2, tri_rows)
    for c in range(n_chunks):
        cl = slice(c * CHUNK, (c + 1) * CHUNK)
        csr_ref[c] = jnp.concatenate([cs_t[0:SUBLANES, cl], cs_t[n_pair:n_pair + SUBLANES, cl]], axis=1)
    e0, e1, e2 = _split3(jnp.exp(cs))
    spread = _dot(jnp.concatenate([dt.astype(BF16), jnp.exp(cs_end - cs).astype(BF16), e0, e1, e2], axis=0), expand)
    xc = xbc_ref[:, 0:sd] * spread[0:tb]
    xc_ref[...] = xc.astype(BF16)
    xe_ref[...] = (xc * spread[tb:2 * tb]).astype(BF16)
    ecs_ref[...] = spread[2 * tb:3 * tb] + spread[3 * tb:4 * tb] + spread[4 * tb:5 * tb]

    def chunk_step(c, carry):
        r0 = pl.multiple_of(c * CHUNK, CHUNK)
        rows = pl.ds(r0, CHUNK)

        gate = _silu(pr_ref[rows, c_g:c_g + hk])
        nb = CHUNK // SUB
        lane64 = _iota2((CHUNK, CHUNK), 1)
        for h in range(HGRN_HEADS):
            hs = slice(h * HGRN_DK, (h + 1) * HGRN_DK)
            qd = qd_ref[rows, hs]
            kd = kd_ref[rows, hs]
            vi = pr_ref[rows, c_i + h * HGRN_DK:c_i + (h + 1) * HGRN_DK].astype(BF16)
            lhs = jnp.concatenate(
                [qx_ref[i - j - 1, pl.ds(r0 + i * SUB, SUB), hs] for j in range(nb - 1) for i in range(j + 1, nb)], axis=0)
            res = _dot_nt(lhs, kx_ref[rows, hs])
            sc = jnp.zeros((CHUNK, CHUNK), F32)
            off = 0
            for j in range(nb - 1):
                n_r = (nb - 1 - j) * SUB
                col = jnp.concatenate([jnp.zeros((CHUNK - n_r, CHUNK), F32), res[off:off + n_r, :]], axis=0)
                sc = jnp.where((lane64 // SUB) == j, col, sc)
                off += n_r
            st = sth_ref[h]
            o_h = (od_ref[pl.ds(h * tb + r0, CHUNK), :] + _dot(sc.astype(BF16), vi)
                   + _dot_nt(qd, st.astype(BF16)))
            sth_ref[h] = st * bd_ref[pl.ds(r0, 1), hs] + _dot_tn(vi, kd)
            o_h = _rms(o_h) * onw_ref[:, hs] * gate[:, hs]
            y_ref[rows, hs] = o_h.astype(BF16)

        xbc = xbc_ref[rows, :]
        xs = xbc[:, 0:sd]
        cs_c = cs_ref[rows, :]
        ecs = ecs_ref[rows, :]
        xcb = xc_ref[rows, :]
        xcd = xe_ref[rows, :]
        cs_rows = csr_ref[c]
        ys = []
        for g in range(SSD_GROUPS):
            bm = xbc[:, sd + g * SSD_STATE:sd + (g + 1) * SSD_STATE].astype(BF16)
            cm = xbc[:, sd + gn + g * SSD_STATE:sd + gn + (g + 1) * SSD_STATE].astype(BF16)
            cbm2 = _dot_nt(cm, jnp.concatenate([bm, bm], axis=0))
            gl = slice(g * hp, (g + 1) * hp)
            st = sts_ref[g]
            y_off = _dot(cm, st.astype(BF16)) * ecs[:, gl]
            sts_ref[g] = st * ecs[CHUNK - 1:CHUNK, gl] + _dot_tn(bm, xcd[:, gl])
            for j in range(SSD_HEADS // SSD_GROUPS // 2):
                p = g * (SSD_HEADS // SSD_GROUPS // 2) + j
                cs_pair = jnp.where(lane2 < CHUNK, cs_c[:, p:p + 1], cs_c[:, n_pair + p:n_pair + p + 1])
                dec = jnp.where(causal2, jnp.exp(cs_pair - cs_rows[p:p + 1, :]), 0.0)
                xp = xcb[:, p * LANES:(p + 1) * LANES]
                rhs = jnp.where(pair_diag, jnp.concatenate([xp, xp], axis=0), jnp.zeros((LANES, LANES), BF16))
                yd = _dot((cbm2 * dec).astype(BF16), rhs)
                ys.append(yd + y_off[:, j * LANES:(j + 1) * LANES])
        y = jnp.concatenate(ys, axis=-1) + xs * dsk_ref[...]
        y = y * _silu(pr_ref[rows, c_z:c_z + sd])
        y = jnp.concatenate([_rms(y[:, g * hp:(g + 1) * hp]) for g in range(SSD_GROUPS)], axis=-1) * snw_ref[...]
        y_ref[rows, hk:hk + sd] = y.astype(BF16)
        return carry

    lax.fori_loop(0, n_chunks, chunk_step, 0, unroll=4)

    o_ref[0] = x + _dot(y_ref[...], wout_ref[...])


def _mixer(x, norm_w, w_in, lb_logits, out_norm_w, conv_w, conv_b, dt_bias, a_log, d_skip, ssd_norm_w,
           w_out, e, *, tb=512, t_step=16):
    b, t, d = x.shape
    n_even, hk = lb_logits.shape
    sd = SSD_HEADS * SSD_HEADDIM
    ab_in = w_in.shape[1]
    pw = -(-ab_in // LANES) * LANES
    n_dt = SSD_HEADS
    order = list(range(0, n_dt, 2)) + list(range(1, n_dt, 2))
    w_dt = w_in[:, ab_in - n_dt:][:, order]
    w_in_p = jnp.pad(jnp.concatenate([w_in[:, :ab_in - n_dt], w_dt], axis=1), ((0, 0), (0, pw - ab_in))).astype(BF16)
    pad8 = lambda a: jnp.pad(a[jnp.array(order)].reshape(1, -1), ((0, 0), (0, LANES - a.shape[-1])))
    taps, cc = conv_w.shape
    n_slab = cc // LANES
    assert tb % HGRN_GROUP == 0 and n_slab == SUBLANES and taps - 1 <= PAD_ROWS
    p_u, p_c = _slab_pitches(PAD_ROWS, tb)
    slab_rows = HGRN_HEADS * (tb // HGRN_GROUP) * GROUP_PITCH
    slab = pltpu.VMEM((slab_rows, LANES), F32)
    return pl.pallas_call(
        functools.partial(_mix_body, e=e, t_step=t_step),
        grid=(b, t // tb),
        in_specs=[
            pl.BlockSpec((1, tb, d), lambda i, j: (i, j, 0)),
            _resident((1, d)),
            _resident((d, pw)),
            _resident((n_even, hk)),
            _resident((1, hk)),
            _resident((taps, n_slab, LANES)),
            _resident((n_slab, LANES)),
            _resident((1, LANES)),
            _resident((1, LANES)),
            _resident((1, sd)),
            _resident((1, sd)),
            _resident((hk + sd, d)),
        ],
        out_specs=pl.BlockSpec((1, tb, d), lambda i, j: (i, j, 0)),
        out_shape=jax.ShapeDtypeStruct((b, t, d), F32),
        scratch_shapes=[
            pltpu.VMEM((tb, pw), F32),
            slab, slab, slab, slab,
            pltpu.VMEM((HGRN_HEADS * tb, LANES), F32),
            pltpu.VMEM((tb, hk), BF16),
            pltpu.VMEM((tb, hk), BF16),
            pltpu.VMEM((tb, hk), F32),
            pltpu.VMEM((CHUNK // SUB - 1, tb, hk), BF16),
            pltpu.VMEM((tb, hk), BF16),
            pltpu.VMEM((n_slab * p_u, LANES), F32),
            pltpu.VMEM((n_slab * p_c, LANES), F32),
            pltpu.VMEM((tb, cc), F32),
            pltpu.VMEM((tb, sd), BF16),
            pltpu.VMEM((tb, sd), BF16),
            pltpu.VMEM((tb, sd), F32),
            pltpu.VMEM((tb, LANES), F32),
            pltpu.VMEM((tb // CHUNK, SUBLANES, LANES), F32),
            pltpu.VMEM((HGRN_HEADS, HGRN_DK, HGRN_DK), F32),
            pltpu.VMEM((SSD_GROUPS, SSD_STATE, sd // SSD_GROUPS), F32),
            pltpu.VMEM((tb, hk + sd), BF16),
        ],
        compiler_params=_params(2),
        name="hgrn_ssd_mixer",
    )(x, norm_w.reshape(1, d), w_in_p, lb_logits, out_norm_w.reshape(1, hk), conv_w.reshape(taps, n_slab, LANES),
      conv_b.reshape(n_slab, LANES),
      pad8(dt_bias), pad8(a_log), jnp.repeat(d_skip, SSD_HEADDIM).reshape(1, sd), ssd_norm_w.reshape(1, sd),
      w_out.astype(BF16))


def kernel(x, mem, mem_norm_w, norm_mix_w, ab_w_in, hgrn_lb_logits, hgrn_out_norm_w, ssd_conv_w, ssd_conv_b,
           ssd_dt_bias, ssd_a_log, ssd_d, ssd_norm_w, ab_w_out, cv_w_pw1, cv_b_pw1, cv_w_dw, cv_b_dw, cv_ln_w,
           cv_ln_b, cv_w_pw2, cv_b_pw2, norm_xattn_w, xattn_wq, xattn_wk, xattn_wv, xattn_wo, norm_mlp_w,
           mlp_w1, mlp_w2, final_norm_w):
    b, t, d = x.shape
    depth = norm_mix_w.shape[0]
    k_all, v_all = _memory_kv(mem, mem_norm_w, xattn_wk, xattn_wv)
    for layer in range(depth):
        if layer % 2 == 0:
            e = layer // 2
            x = _mixer(x, norm_mix_w[layer], ab_w_in[e], hgrn_lb_logits, hgrn_out_norm_w[e], ssd_conv_w[e],
                       ssd_conv_b[e], ssd_dt_bias[e], ssd_a_log[e], ssd_d[e], ssd_norm_w[e], ab_w_out[e], e)
        else:
            o = layer // 2
            x = _conformer(x, norm_mix_w[layer], cv_w_pw1[o], cv_b_pw1[o], cv_w_dw[o], cv_b_dw[o], cv_ln_w[o],
                           cv_ln_b[o], cv_w_pw2[o], cv_b_pw2[o])
        x = _cross_attention(x, norm_xattn_w[layer], xattn_wq[layer], k_all, v_all, xattn_wo[layer], layer)
        x = _mlp(x.reshape(b * t, d), norm_mlp_w[layer], mlp_w1[layer], mlp_w2[layer], final_norm_w,
                 final=(layer == depth - 1)).reshape(b, t, d)
    return x
```

```python
import functools

import jax
import jax.numpy as jnp
from jax import lax
from jax.experimental import pallas as pl
from jax.experimental.pallas import tpu as pltpu

F32 = jnp.float32
BF16 = jnp.bfloat16

EPS = 1e-6
CHUNK = 64
SUB = 16
LANES = 128
SUBLANES = 8
PAD_ROWS = 8

HGRN_HEADS = 4
HGRN_DK = 128
SSD_HEADS = 8
SSD_HEADDIM = 64
SSD_GROUPS = 2
SSD_STATE = 128
XATTN_HEADS = 4

VMEM_LIMIT_BYTES = 56 * 1024 * 1024


def _rms(x):
    return x * lax.rsqrt(jnp.mean(x * x, axis=-1, keepdims=True) + EPS)


def _sigmoid(x):
    return 1.0 / (1.0 + jnp.exp(-x))


def _silu(x):
    return x * _sigmoid(x)


def _dot(a, b):
    return jnp.dot(a, b, preferred_element_type=F32)


def _dot_nt(a, b):
    return lax.dot_general(a, b, (((1,), (1,)), ((), ())), preferred_element_type=F32)


def _dot_tn(a, b):
    return lax.dot_general(a, b, (((0,), (0,)), ((), ())), preferred_element_type=F32)


def _split3(a):
    a0 = a.astype(BF16)
    r1 = a - a0.astype(F32)
    a1 = r1.astype(BF16)
    a2 = (r1 - a1.astype(F32)).astype(BF16)
    return a0, a1, a2


def _dot01(m01, a):
    a0, a1, a2 = _split3(a)
    return _dot(m01, a0) + _dot(m01, a1) + _dot(m01, a2)


def _resident(shape):
    nd = len(shape)
    return pl.BlockSpec(shape, lambda *_: (0,) * nd, pipeline_mode=pl.Buffered(1))


def _params(n_axes):
    return pltpu.CompilerParams(dimension_semantics=("arbitrary",) * n_axes,
                                vmem_limit_bytes=VMEM_LIMIT_BYTES)


def _kv_body(mem_ref, mw_ref, wk_ref, wv_ref, k_ref, v_ref):
    m = mem_ref[0]
    mn = (_rms(m) * mw_ref[...]).astype(BF16)
    k_ref[0, 0] = _dot(mn, wk_ref[0].astype(BF16)).astype(BF16)
    v_ref[0, 0] = _dot(mn, wv_ref[0].astype(BF16)).astype(BF16)


def _memory_kv(mem, mem_norm_w, wk, wv):
    depth, d, _ = wk.shape
    b, m, _ = mem.shape
    out = jax.ShapeDtypeStruct((depth, b, m, d), BF16)
    return pl.pallas_call(
        _kv_body,
        grid=(depth, b),
        in_specs=[
            pl.BlockSpec((1, m, d), lambda l, i: (i, 0, 0)),
            pl.BlockSpec((1, d), lambda l, i: (0, 0)),
            pl.BlockSpec((1, d, d), lambda l, i: (l, 0, 0)),
            pl.BlockSpec((1, d, d), lambda l, i: (l, 0, 0)),
        ],
        out_specs=[
            pl.BlockSpec((1, 1, m, d), lambda l, i: (l, i, 0, 0)),
            pl.BlockSpec((1, 1, m, d), lambda l, i: (l, i, 0, 0)),
        ],
        out_shape=[out, out],
        compiler_params=_params(2),
        name="memory_kv",
    )(mem, mem_norm_w.reshape(1, d), wk, wv)


def _mlp_body(x_ref, nw_ref, w1_ref, w2_ref, fw_ref, o_ref, *, ff_chunk, final):
    x = x_ref[...]
    hb = (_rms(x) * nw_ref[...]).astype(BF16)
    acc = x
    for c in range(w1_ref.shape[1] // ff_chunk):
        a = _dot(hb, w1_ref[:, c * ff_chunk:(c + 1) * ff_chunk])
        a = jnp.square(jnp.maximum(a, 0.0)).astype(BF16)
        acc = acc + _dot(a, w2_ref[c * ff_chunk:(c + 1) * ff_chunk, :])
    if final:
        acc = _rms(acc) * fw_ref[...]
    o_ref[...] = acc


def _mlp(x2, norm_w, w1, w2, final_w, *, final, tb=1024, ff_chunk=1024):
    n, d = x2.shape
    dff = w1.shape[1]
    return pl.pallas_call(
        functools.partial(_mlp_body, ff_chunk=ff_chunk, final=final),
        grid=(n // tb,),
        in_specs=[
            pl.BlockSpec((tb, d), lambda i: (i, 0)),
            _resident((1, d)),
            _resident((d, dff)),
            _resident((dff, d)),
            _resident((1, d)),
        ],
        out_specs=pl.BlockSpec((tb, d), lambda i: (i, 0)),
        out_shape=jax.ShapeDtypeStruct((n, d), F32),
        compiler_params=_params(1),
        name="sq_relu_mlp",
    )(x2, norm_w.reshape(1, d), w1.astype(BF16), w2.astype(BF16), final_w.reshape(1, d))


def _attn_body(x_ref, nw_ref, wq_ref, k_ref, v_ref, wo_ref, o_ref, *, heads):
    x = x_ref[0]
    d = x.shape[-1]
    hd = d // heads
    hb = (_rms(x) * nw_ref[...]).astype(BF16)
    q = (_dot(hb, wq_ref[...]) * (hd ** -0.5)).astype(BF16)
    outs = []
    for h in range(heads):
        sl = slice(h * hd, (h + 1) * hd)
        s = _dot_nt(q[:, sl], k_ref[0, 0, :, sl])
        p = jnp.exp(s - jnp.max(s, axis=-1, keepdims=True))
        den = jnp.sum(p, axis=-1, keepdims=True)
        oh = _dot(p.astype(BF16), v_ref[0, 0, :, sl]) * (1.0 / den)
        outs.append(oh.astype(BF16))
    o = jnp.concatenate(outs, axis=-1)
    o_ref[0] = x + _dot(o, wo_ref[...])


def _cross_attention(x, norm_w, wq, k_all, v_all, wo, layer, *, tb=1024):
    b, t, d = x.shape
    m = k_all.shape[2]
    return pl.pallas_call(
        functools.partial(_attn_body, heads=XATTN_HEADS),
        grid=(b, t // tb),
        in_specs=[
            pl.BlockSpec((1, tb, d), lambda i, j: (i, j, 0)),
            _resident((1, d)),
            _resident((d, d)),
            pl.BlockSpec((1, 1, m, d), lambda i, j: (layer, i, 0, 0)),
            pl.BlockSpec((1, 1, m, d), lambda i, j: (layer, i, 0, 0)),
            _resident((d, d)),
        ],
        out_specs=pl.BlockSpec((1, tb, d), lambda i, j: (i, j, 0)),
        out_shape=jax.ShapeDtypeStruct((b, t, d), F32),
        compiler_params=_params(2),
        name="memory_xattn",
    )(x, norm_w.reshape(1, d), wq.astype(BF16), k_all, v_all, wo.astype(BF16))


def _slab_conv(u_ref, c_ref, w_taps, bias, *, hist, n_t, t_step):
    taps = len(w_taps)
    n_slab = bias.shape[0]
    p_u = u_ref.shape[0] // n_slab
    p_c = c_ref.shape[0] // n_slab

    def step(i, carry):
        t0 = pl.multiple_of(i * t_step, t_step)
        accs = [bias] * t_step
        for k in range(t_step + taps - 1):
            uk = u_ref[pl.ds(t0 + hist - (taps - 1) + k, n_slab, stride=p_u), :]
            for tt in range(t_step):
                if 0 <= k - tt < taps:
                    accs[tt] = accs[tt] + w_taps[k - tt] * uk
        for tt in range(t_step):
            c_ref[pl.ds(t0 + tt, n_slab, stride=p_c), :] = accs[tt]
        return carry

    lax.fori_loop(0, n_t // t_step, step, 0)


def _slab_pitches(hist, n_t):
    return hist + n_t + 4, n_t + SUBLANES


def _conf_body(x_ref, nw_ref, w1_ref, b1_ref, wdw_ref, bdw_ref, lnw_ref, lnb_ref, w2_ref, b2_ref,
               o_ref, u_ref, c_ref, *, taps, hist, t_step):
    tb, d = x_ref.shape[1], x_ref.shape[2]
    n_slab = d // LANES
    p_u = u_ref.shape[0] // n_slab
    p_c = c_ref.shape[0] // n_slab

    @pl.when(pl.program_id(1) == 0)
    def _():
        for s in range(n_slab):
            u_ref[s * p_u:s * p_u + hist, :] = jnp.zeros((hist, LANES), F32)

    x = x_ref[0]
    hb = (_rms(x) * nw_ref[...]).astype(BF16)
    ag = _dot(hb, w1_ref[...]) + b1_ref[...]
    u = ag[:, :d] * _sigmoid(ag[:, d:])
    for s in range(n_slab):
        u_ref[s * p_u + hist:s * p_u + hist + tb, :] = u[:, s * LANES:(s + 1) * LANES]

    _slab_conv(u_ref, c_ref, [wdw_ref[j] for j in range(taps)], bdw_ref[...], hist=hist, n_t=tb, t_step=t_step)
    for s in range(n_slab):
        u_ref[s * p_u:s * p_u + hist, :] = u_ref[s * p_u + tb:s * p_u + tb + hist, :]

    c = jnp.concatenate([c_ref[s * p_c:s * p_c + tb, :] for s in range(n_slab)], axis=-1)
    mu = jnp.mean(c, axis=-1, keepdims=True)
    cc = c - mu
    var = jnp.mean(cc * cc, axis=-1, keepdims=True)
    yn = cc * lax.rsqrt(var + EPS) * lnw_ref[...] + lnb_ref[...]
    ys = _silu(yn).astype(BF16)
    o_ref[0] = x + _dot(ys, w2_ref[...]) + b2_ref[...]


def _conformer(x, norm_w, w_pw1, b_pw1, w_dw, b_dw, ln_w, ln_b, w_pw2, b_pw2, *, tb=512, t_step=16):
    b, t, d = x.shape
    taps = w_dw.shape[0]
    n_slab = d // LANES
    assert n_slab == SUBLANES
    hist = -(-(taps - 1) // SUBLANES) * SUBLANES
    p_u, p_c = _slab_pitches(hist, tb)
    return pl.pallas_call(
        functools.partial(_conf_body, taps=taps, hist=hist, t_step=t_step),
        grid=(b, t // tb),
        in_specs=[
            pl.BlockSpec((1, tb, d), lambda i, j: (i, j, 0)),
            _resident((1, d)),
            _resident((d, 2 * d)),
            _resident((1, 2 * d)),
            _resident((taps, n_slab, LANES)),
            _resident((n_slab, LANES)),
            _resident((1, d)),
            _resident((1, d)),
            _resident((d, d)),
            _resident((1, d)),
        ],
        out_specs=pl.BlockSpec((1, tb, d), lambda i, j: (i, j, 0)),
        out_shape=jax.ShapeDtypeStruct((b, t, d), F32),
        scratch_shapes=[pltpu.VMEM((n_slab * p_u, LANES), F32), pltpu.VMEM((n_slab * p_c, LANES), F32)],
        compiler_params=_params(2),
        name="conformer_conv",
    )(x, norm_w.reshape(1, d), w_pw1.astype(BF16), b_pw1.reshape(1, 2 * d), w_dw.reshape(taps, n_slab, LANES),
      b_dw.reshape(n_slab, LANES), ln_w.reshape(1, d), ln_b.reshape(1, d), w_pw2.astype(BF16), b_pw2.reshape(1, d))


def _iota2(shape, axis):
    return lax.broadcasted_iota(jnp.int32, shape, axis)


HGRN_GROUP = SUB * SUBLANES
BLOCK_PITCH = SUB + 4
GROUP_PITCH = SUBLANES * BLOCK_PITCH
LOG2E = 1.4426950408889634
LOCAL_UNROLL = 8


def _mix_body(x_ref, nw_ref, win_ref, lbl_ref, onw_ref, cw_ref, cb_ref, dtb_ref, alog_ref, dsk_ref,
              snw_ref, wout_ref, o_ref,
              pr_ref, qs_ref, ks_ref, bs_ref, vs_ref, od_ref, qd_ref, kd_ref, bd_ref, qx_ref, kx_ref,
              su_ref, sc_ref, xbc_ref, xc_ref, xe_ref, ecs_ref, cs_ref, csr_ref, yd_ref, oc_ref, sth_ref, sts_ref, y_ref, *, e, t_step):
    tb, d = x_ref.shape[1], x_ref.shape[2]
    hk = HGRN_HEADS * HGRN_DK
    sd = SSD_HEADS * SSD_HEADDIM
    gn = SSD_GROUPS * SSD_STATE
    c_q, c_f, c_i, c_g = 0, hk, 2 * hk, 3 * hk
    c_z = 4 * hk
    c_x = c_z + sd
    c_dt = c_x + sd + 2 * gn
    hp = SSD_HEADS // SSD_GROUPS * SSD_HEADDIM
    taps = cw_ref.shape[0]
    hist = PAD_ROWS
    n_slab = cb_ref.shape[0]
    p_u = su_ref.shape[0] // n_slab
    p_c = sc_ref.shape[0] // n_slab
    n_groups = tb // HGRN_GROUP
    n_chunks = tb // CHUNK

    @pl.when(pl.program_id(1) == 0)
    def _():
        for s in range(n_slab):
            su_ref[s * p_u:s * p_u + hist, :] = jnp.zeros((hist, LANES), F32)
        sth_ref[...] = jnp.zeros(sth_ref.shape, F32)
        sts_ref[...] = jnp.zeros(sts_ref.shape, F32)

    x = x_ref[0]
    hb = (_rms(x) * nw_ref[...]).astype(BF16)
    pr_ref[:, 0:c_g] = _dot(hb, win_ref[:, 0:c_g])

    lg = lbl_ref[...]
    ex = jnp.exp(lg - jnp.max(lg, axis=0, keepdims=True))
    probs = ex / jnp.sum(ex, axis=0, keepdims=True)
    lb = jnp.sum(probs[0:e + 1], axis=0, keepdims=True) - probs[0:1]

    rg = _iota2((HGRN_GROUP, HGRN_GROUP), 0)
    cg = _iota2((HGRN_GROUP, HGRN_GROUP), 1)
    same_chunk = (rg // CHUNK) == (cg // CHUNK)
    m01 = lambda cond: jnp.where(same_chunk & cond, 1.0, 0.0).astype(BF16)
    tri_incl = m01(cg <= rg)
    r64 = _iota2((CHUNK, CHUNK), 0)
    c64 = _iota2((CHUNK, CHUNK), 1)
    tri64 = jnp.where(c64 <= r64, 1.0, 0.0).astype(BF16)
    lane2 = _iota2((CHUNK, LANES), 1)
    causal2 = (lane2 % CHUNK) <= _iota2((CHUNK, LANES), 0)
    rb = _iota2((tb, tb), 0)
    cb_ = _iota2((tb, tb), 1)
    tri_rows = jnp.where(((rb // CHUNK) == (cb_ // CHUNK)) & (rb <= cb_), 1.0, 0.0).astype(BF16)
    n_pair = SSD_HEADS // 2
    er = _iota2((LANES, sd), 0)
    ehead = jnp.where(er < n_pair, 2 * er, 2 * (er - n_pair) + 1)
    expand = jnp.where((_iota2((LANES, sd), 1) // SSD_HEADDIM == ehead) & (er < SSD_HEADS), 1.0, 0.0).astype(BF16)
    ones_k = jnp.ones((HGRN_DK, HGRN_DK), BF16)
    pair_diag = (_iota2((LANES, LANES), 0) // SSD_HEADDIM) == (_iota2((LANES, LANES), 1) // SSD_HEADDIM)

    for g in range(n_groups):
        rows = slice(g * HGRN_GROUP, (g + 1) * HGRN_GROUP)
        orow = rows
        fr = pr_ref[rows, c_f:c_f + hk]
        lf = jnp.log(lb + (1.0 - lb) * _sigmoid(fr)) * LOG2E
        kk = (1.0 - lb) * _sigmoid(-fr)
        bcs = _dot01(tri_incl, lf)
        nb = CHUNK // SUB
        n_blk = HGRN_GROUP // SUB
        ends = [bcs[(i + 1) * SUB - 1:(i + 1) * SUB, :] for i in range(n_blk)]
        zero_row = jnp.zeros((1, hk), F32)

        def at_block_end(back):
            rows_ = [ends[i - back] if (i % nb) - back >= 0 else zero_row for i in range(n_blk)]
            return jnp.concatenate([jnp.broadcast_to(r_, (SUB, hk)) for r_ in rows_], axis=0)

        btot = jnp.concatenate([jnp.broadcast_to(ends[(i // nb) * nb + nb - 1], (SUB, hk)) for i in range(n_blk)], axis=0)
        qq = pr_ref[rows, c_q:c_q + hk] * (HGRN_DK ** -0.5)
        vv = pr_ref[rows, c_i:c_i + hk]
        qd_ref[orow, :] = (qq * jnp.exp2(bcs)).astype(BF16)
        kd_ref[orow, :] = (kk * jnp.exp2(btot - bcs)).astype(BF16)
        bd_ref[orow, :] = jnp.exp2(btot)
        kx_ref[orow, :] = (kk * jnp.exp2(at_block_end(0) - bcs)).astype(BF16)
        for m in range(1, nb):
            qx_ref[m - 1, orow, :] = (qq * jnp.exp2(bcs - at_block_end(m))).astype(BF16)
        for h in range(HGRN_HEADS):
            hs = slice(h * HGRN_DK, (h + 1) * HGRN_DK)
            for blk in range(SUBLANES):
                src = slice(blk * SUB, (blk + 1) * SUB)
                lo = (h * n_groups + g) * GROUP_PITCH + blk * BLOCK_PITCH
                qs_ref[lo:lo + SUB, :] = qq[src, hs]
                ks_ref[lo:lo + SUB, :] = kk[src, hs]
                bs_ref[lo:lo + SUB, :] = bcs[src, hs]
                vs_ref[lo:lo + SUB, :] = vv[src, hs]

    pr_ref[:, c_g:] = _dot(hb, win_ref[:, c_g:])

    def pair_step(idx, carry):
        sb = pl.multiple_of(idx * GROUP_PITCH, SUBLANES)
        ob = pl.multiple_of(idx * HGRN_GROUP, HGRN_GROUP)
        ld = lambda ref, r: ref[pl.ds(sb + r, SUBLANES, stride=BLOCK_PITCH), :]
        qv = [ld(qs_ref, r) for r in range(SUB)]
        kv = [ld(ks_ref, r) for r in range(SUB)]
        bv = [ld(bs_ref, r) for r in range(SUB)]
        vv_ = [ld(vs_ref, r) for r in range(SUB)]
        prods = []
        for r in range(SUB):
            for s_ in range(r + 1):
                if s_ == r:
                    prods.append(qv[r] * kv[r])
                else:
                    prods.append(qv[r] * kv[s_] * jnp.exp2(bv[r] - bv[s_]))
        sc = _dot(jnp.concatenate(prods, axis=0).astype(BF16), ones_k)
        n = 0
        for r in range(SUB):
            acc = None
            for s_ in range(r + 1):
                term = sc[n * SUBLANES:(n + 1) * SUBLANES, :] * vv_[s_]
                acc = term if acc is None else acc + term
                n += 1
            od_ref[pl.ds(ob + r, SUBLANES, stride=SUB), :] = acc
        return carry

    lax.fori_loop(0, HGRN_HEADS * n_groups, pair_step, 0, unroll=2)

    for s in range(n_slab):
        su_ref[s * p_u + hist:s * p_u + hist + tb, :] = pr_ref[:, c_x + s * LANES:c_x + (s + 1) * LANES]
    _slab_conv(su_ref, sc_ref, [cw_ref[j] for j in range(taps)], cb_ref[...], hist=hist, n_t=tb, t_step=t_step)
    for s in range(n_slab):
        su_ref[s * p_u:s * p_u + hist, :] = su_ref[s * p_u + tb:s * p_u + tb + hist, :]
    xbc_ref[...] = _silu(jnp.concatenate([sc_ref[s * p_c:s * p_c + tb, :] for s in range(n_slab)], axis=-1))
    dt = jnp.logaddexp(pr_ref[:, c_dt:c_dt + LANES] + dtb_ref[...], 0.0)
    da = dt * (-jnp.exp(alog_ref[...]))
    d0, d1, d2 = _split3(da)
    for c in range(n_chunks):
        cr = slice(c * CHUNK, (c + 1) * CHUNK)
        cs_ref[cr, :] = _dot(tri64, d0[cr]) + _dot(tri64, d1[cr]) + _dot(tri64, d2[cr])
    cs = cs_ref[...]
    cs_end = jnp.concatenate(
        [jnp.broadcast_to(cs[(c + 1) * CHUNK - 1:(c + 1) * CHUNK, :], (CHUNK, LANES)) for c in range(n_chunks)], axis=0)
    cs_t = _dot_tn(d0, tri_rows) + _dot_tn(d1, tri_rows) + _dot_tn(d2, tri_rows)
    for c in range(n_chunks):
        cl = slice(c * CHUNK, (c + 1) * CHUNK)
        csr_ref[c] = jnp.concatenate([cs_t[0:SUBLANES, cl], cs_t[n_pair:n_pair + SUBLANES, cl]], axis=1)
    e0, e1, e2 = _split3(jnp.exp(cs))
    spread = _dot(jnp.concatenate([dt.astype(BF16), jnp.exp(cs_end - cs).astype(BF16), e0, e1, e2], axis=0), expand)
    xc = xbc_ref[:, 0:sd] * spread[0:tb]
    xc_ref[...] = xc.astype(BF16)
    xe_ref[...] = (xc * spread[tb:2 * tb]).astype(BF16)
    ecs_ref[...] = spread[2 * tb:3 * tb] + spread[3 * tb:4 * tb] + spread[4 * tb:5 * tb]

    def local_step(it, carry):
        nb = CHUNK // SUB
        lane64 = _iota2((CHUNK, CHUNK), 1)
        chunks = [it * LOCAL_UNROLL + u for u in range(LOCAL_UNROLL)]
        r0s = [pl.multiple_of(c * CHUNK, CHUNK) for c in chunks]
        stage1 = []
        for c, r0 in zip(chunks, r0s):
            rows = pl.ds(r0, CHUNK)
            res = []
            for h in range(HGRN_HEADS):
                hs = slice(h * HGRN_DK, (h + 1) * HGRN_DK)
                lhs = jnp.concatenate(
                    [qx_ref[i - j - 1, pl.ds(r0 + i * SUB, SUB), hs] for j in range(nb - 1) for i in range(j + 1, nb)],
                    axis=0)
                res.append(_dot_nt(lhs, kx_ref[rows, hs]))
            cbm2 = []
            for g in range(SSD_GROUPS):
                bm = xbc_ref[rows, sd + g * SSD_STATE:sd + (g + 1) * SSD_STATE].astype(BF16)
                cm = xbc_ref[rows, sd + gn + g * SSD_STATE:sd + gn + (g + 1) * SSD_STATE].astype(BF16)
                cbm2.append(_dot_nt(cm, jnp.concatenate([bm, bm], axis=0)))
            stage1.append((res, cbm2))
        for (c, r0), (res, cbm2) in zip(zip(chunks, r0s), stage1):
            rows = pl.ds(r0, CHUNK)
            for h in range(HGRN_HEADS):
                vi = pr_ref[rows, c_i + h * HGRN_DK:c_i + (h + 1) * HGRN_DK].astype(BF16)
                sc = jnp.zeros((CHUNK, CHUNK), F32)
                off = 0
                for j in range(nb - 1):
                    n_r = (nb - 1 - j) * SUB
                    col = jnp.concatenate([jnp.zeros((CHUNK - n_r, CHUNK), F32), res[h][off:off + n_r, :]], axis=0)
                    sc = jnp.where((lane64 // SUB) == j, col, sc)
                    off += n_r
                oc_ref[pl.ds(h * tb + r0, CHUNK), :] = _dot(sc.astype(BF16), vi)
            cs_c = cs_ref[rows, :]
            cs_rows = csr_ref[c]
            for p in range(n_pair):
                cs_pair = jnp.where(lane2 < CHUNK, cs_c[:, p:p + 1], cs_c[:, n_pair + p:n_pair + p + 1])
                dec = jnp.where(causal2, jnp.exp(cs_pair - cs_rows[p:p + 1, :]), 0.0)
                xp = xc_ref[rows, p * LANES:(p + 1) * LANES]
                rhs = jnp.where(pair_diag, jnp.concatenate([xp, xp], axis=0), jnp.zeros((LANES, LANES), BF16))
                g = p // (n_pair // SSD_GROUPS)
                yd_ref[rows, p * LANES:(p + 1) * LANES] = _dot((cbm2[g] * dec).astype(BF16), rhs)
        return carry

    lax.fori_loop(0, n_chunks // LOCAL_UNROLL, local_step, 0)

    def state_step(c, carry):
        r0 = pl.multiple_of(c * CHUNK, CHUNK)
        rows = pl.ds(r0, CHUNK)
        gate = _silu(pr_ref[rows, c_g:c_g + hk])
        for h in range(HGRN_HEADS):
            hs = slice(h * HGRN_DK, (h + 1) * HGRN_DK)
            vi = pr_ref[rows, c_i + h * HGRN_DK:c_i + (h + 1) * HGRN_DK].astype(BF16)
            st = sth_ref[h]
            orow = pl.ds(h * tb + r0, CHUNK)
            o_h = od_ref[orow, :] + oc_ref[orow, :] + _dot_nt(qd_ref[rows, hs], st.astype(BF16))
            sth_ref[h] = st * bd_ref[pl.ds(r0, 1), hs] + _dot_tn(vi, kd_ref[rows, hs])
            o_h = _rms(o_h) * onw_ref[:, hs] * gate[:, hs]
            y_ref[rows, hs] = o_h.astype(BF16)

        xbc = xbc_ref[rows, :]
        ecs = ecs_ref[rows, :]
        xcd = xe_ref[rows, :]
        ys = []
        for g in range(SSD_GROUPS):
            bm = xbc[:, sd + g * SSD_STATE:sd + (g + 1) * SSD_STATE].astype(BF16)
            cm = xbc[:, sd + gn + g * SSD_STATE:sd + gn + (g + 1) * SSD_STATE].astype(BF16)
            gl = slice(g * hp, (g + 1) * hp)
            st = sts_ref[g]
            ys.append(_dot(cm, st.astype(BF16)) * ecs[:, gl])
            sts_ref[g] = st * ecs[CHUNK - 1:CHUNK, gl] + _dot_tn(bm, xcd[:, gl])
        y = jnp.concatenate(ys, axis=-1) + yd_ref[rows, :] + xbc[:, 0:sd] * dsk_ref[...]
        y = y * _silu(pr_ref[rows, c_z:c_z + sd])
        y = jnp.concatenate([_rms(y[:, g * hp:(g + 1) * hp]) for g in range(SSD_GROUPS)], axis=-1) * snw_ref[...]
        y_ref[rows, hk:hk + sd] = y.astype(BF16)
        return carry

    lax.fori_loop(0, n_chunks, state_step, 0, unroll=4)

    o_ref[0] = x + _dot(y_ref[...], wout_ref[...])


def _mixer(x, norm_w, w_in, lb_logits, out_norm_w, conv_w, conv_b, dt_bias, a_log, d_skip, ssd_norm_w,
           w_out, e, *, tb=512, t_step=16):
    b, t, d = x.shape
    n_even, hk = lb_logits.shape
    sd = SSD_HEADS * SSD_HEADDIM
    ab_in = w_in.shape[1]
    pw = -(-ab_in // LANES) * LANES
    n_dt = SSD_HEADS
    order = list(range(0, n_dt, 2)) + list(range(1, n_dt, 2))
    w_dt = w_in[:, ab_in - n_dt:][:, order]
    w_in_p = jnp.pad(jnp.concatenate([w_in[:, :ab_in - n_dt], w_dt], axis=1), ((0, 0), (0, pw - ab_in))).astype(BF16)
    pad8 = lambda a: jnp.pad(a[jnp.array(order)].reshape(1, -1), ((0, 0), (0, LANES - a.shape[-1])))
    taps, cc = conv_w.shape
    n_slab = cc // LANES
    assert tb % HGRN_GROUP == 0 and n_slab == SUBLANES and taps - 1 <= PAD_ROWS
    p_u, p_c = _slab_pitches(PAD_ROWS, tb)
    slab_rows = HGRN_HEADS * (tb // HGRN_GROUP) * GROUP_PITCH
    slab = pltpu.VMEM((slab_rows, LANES), F32)
    return pl.pallas_call(
        functools.partial(_mix_body, e=e, t_step=t_step),
        grid=(b, t // tb),
        in_specs=[
            pl.BlockSpec((1, tb, d), lambda i, j: (i, j, 0)),
            _resident((1, d)),
            _resident((d, pw)),
            _resident((n_even, hk)),
            _resident((1, hk)),
            _resident((taps, n_slab, LANES)),
            _resident((n_slab, LANES)),
            _resident((1, LANES)),
            _resident((1, LANES)),
            _resident((1, sd)),
            _resident((1, sd)),
            _resident((hk + sd, d)),
        ],
        out_specs=pl.BlockSpec((1, tb, d), lambda i, j: (i, j, 0)),
        out_shape=jax.ShapeDtypeStruct((b, t, d), F32),
        scratch_shapes=[
            pltpu.VMEM((tb, pw), F32),
            slab, slab, slab, slab,
            pltpu.VMEM((HGRN_HEADS * tb, LANES), F32),
            pltpu.VMEM((tb, hk), BF16),
            pltpu.VMEM((tb, hk), BF16),
            pltpu.VMEM((tb, hk), F32),
            pltpu.VMEM((CHUNK // SUB - 1, tb, hk), BF16),
            pltpu.VMEM((tb, hk), BF16),
            pltpu.VMEM((n_slab * p_u, LANES), F32),
            pltpu.VMEM((n_slab * p_c, LANES), F32),
            pltpu.VMEM((tb, cc), F32),
            pltpu.VMEM((tb, sd), BF16),
            pltpu.VMEM((tb, sd), BF16),
            pltpu.VMEM((tb, sd), F32),
            pltpu.VMEM((tb, LANES), F32),
            pltpu.VMEM((tb // CHUNK, SUBLANES, LANES), F32),
            pltpu.VMEM((tb, sd), F32),
            pltpu.VMEM((HGRN_HEADS * tb, LANES), F32),
            pltpu.VMEM((HGRN_HEADS, HGRN_DK, HGRN_DK), F32),
            pltpu.VMEM((SSD_GROUPS, SSD_STATE, sd // SSD_GROUPS), F32),
            pltpu.VMEM((tb, hk + sd), BF16),
        ],
        compiler_params=_params(2),
        name="hgrn_ssd_mixer",
    )(x, norm_w.reshape(1, d), w_in_p, lb_logits, out_norm_w.reshape(1, hk), conv_w.reshape(taps, n_slab, LANES),
      conv_b.reshape(n_slab, LANES),
      pad8(dt_bias), pad8(a_log), jnp.repeat(d_skip, SSD_HEADDIM).reshape(1, sd), ssd_norm_w.reshape(1, sd),
      w_out.astype(BF16))


def kernel(x, mem, mem_norm_w, norm_mix_w, ab_w_in, hgrn_lb_logits, hgrn_out_norm_w, ssd_conv_w, ssd_conv_b,
           ssd_dt_bias, ssd_a_log, ssd_d, ssd_norm_w, ab_w_out, cv_w_pw1, cv_b_pw1, cv_w_dw, cv_b_dw, cv_ln_w,
           cv_ln_b, cv_w_pw2, cv_b_pw2, norm_xattn_w, xattn_wq, xattn_wk, xattn_wv, xattn_wo, norm_mlp_w,
           mlp_w1, mlp_w2, final_norm_w):
    b, t, d = x.shape
    depth = norm_mix_w.shape[0]
    k_all, v_all = _memory_kv(mem, mem_norm_w, xattn_wk, xattn_wv)
    for layer in range(depth):
        if layer % 2 == 0:
            e = layer // 2
            x = _mixer(x, norm_mix_w[layer], ab_w_in[e], hgrn_lb_logits, hgrn_out_norm_w[e], ssd_conv_w[e],
                       ssd_conv_b[e], ssd_dt_bias[e], ssd_a_log[e], ssd_d[e], ssd_norm_w[e], ab_w_out[e], e)
        else:
            o = layer // 2
            x = _conformer(x, norm_mix_w[layer], cv_w_pw1[o], cv_b_pw1[o], cv_w_dw[o], cv_b_dw[o], cv_ln_w[o],
                           cv_ln_b[o], cv_w_pw2[o], cv_b_pw2[o])
        x = _cross_attention(x, norm_xattn_w[layer], xattn_wq[layer], k_all, v_all, xattn_wo[layer], layer)
        x = _mlp(x.reshape(b * t, d), norm_mlp_w[layer], mlp_w1[layer], mlp_w2[layer], final_norm_w,
                 final=(layer == depth - 1)).reshape(b, t, d)
    return x
```

```python
import functools

import jax
import jax.numpy as jnp
from jax import lax
from jax.experimental import pallas as pl
from jax.experimental.pallas import tpu as pltpu

F32 = jnp.float32
BF16 = jnp.bfloat16

EPS = 1e-6
CHUNK = 64
SUB = 16
LANES = 128
SUBLANES = 8
PAD_ROWS = 8

HGRN_HEADS = 4
HGRN_DK = 128
SSD_HEADS = 8
SSD_HEADDIM = 64
SSD_GROUPS = 2
SSD_STATE = 128
XATTN_HEADS = 4

VMEM_LIMIT_BYTES = 56 * 1024 * 1024


def _rms(x):
    return x * lax.rsqrt(jnp.mean(x * x, axis=-1, keepdims=True) + EPS)


def _sigmoid(x):
    return 1.0 / (1.0 + jnp.exp(-x))


def _silu(x):
    return x * _sigmoid(x)


def _dot(a, b):
    return jnp.dot(a, b, preferred_element_type=F32)


def _dot_nt(a, b):
    return lax.dot_general(a, b, (((1,), (1,)), ((), ())), preferred_element_type=F32)


def _dot_tn(a, b):
    return lax.dot_general(a, b, (((0,), (0,)), ((), ())), preferred_element_type=F32)


def _split3(a):
    a0 = a.astype(BF16)
    r1 = a - a0.astype(F32)
    a1 = r1.astype(BF16)
    a2 = (r1 - a1.astype(F32)).astype(BF16)
    return a0, a1, a2


def _dot01(m01, a):
    a0, a1, a2 = _split3(a)
    return _dot(m01, a0) + _dot(m01, a1) + _dot(m01, a2)


def _resident(shape):
    nd = len(shape)
    return pl.BlockSpec(shape, lambda *_: (0,) * nd, pipeline_mode=pl.Buffered(1))


def _params(n_axes):
    return pltpu.CompilerParams(dimension_semantics=("arbitrary",) * n_axes,
                                vmem_limit_bytes=VMEM_LIMIT_BYTES)


def _kv_body(mem_ref, mw_ref, wk_ref, wv_ref, k_ref, v_ref):
    m = mem_ref[0]
    mn = (_rms(m) * mw_ref[...]).astype(BF16)
    k_ref[0, 0] = _dot(mn, wk_ref[0].astype(BF16)).astype(BF16)
    v_ref[0, 0] = _dot(mn, wv_ref[0].astype(BF16)).astype(BF16)


def _memory_kv(mem, mem_norm_w, wk, wv):
    depth, d, _ = wk.shape
    b, m, _ = mem.shape
    out = jax.ShapeDtypeStruct((depth, b, m, d), BF16)
    return pl.pallas_call(
        _kv_body,
        grid=(depth, b),
        in_specs=[
            pl.BlockSpec((1, m, d), lambda l, i: (i, 0, 0)),
            pl.BlockSpec((1, d), lambda l, i: (0, 0)),
            pl.BlockSpec((1, d, d), lambda l, i: (l, 0, 0)),
            pl.BlockSpec((1, d, d), lambda l, i: (l, 0, 0)),
        ],
        out_specs=[
            pl.BlockSpec((1, 1, m, d), lambda l, i: (l, i, 0, 0)),
            pl.BlockSpec((1, 1, m, d), lambda l, i: (l, i, 0, 0)),
        ],
        out_shape=[out, out],
        compiler_params=_params(2),
        name="memory_kv",
    )(mem, mem_norm_w.reshape(1, d), wk, wv)


def _mlp_body(x_ref, nw_ref, w1_ref, w2_ref, fw_ref, o_ref, *, ff_chunk, final):
    x = x_ref[...]
    hb = (_rms(x) * nw_ref[...]).astype(BF16)
    acc = x
    for c in range(w1_ref.shape[1] // ff_chunk):
        a = _dot(hb, w1_ref[:, c * ff_chunk:(c + 1) * ff_chunk])
        a = jnp.square(jnp.maximum(a, 0.0)).astype(BF16)
        acc = acc + _dot(a, w2_ref[c * ff_chunk:(c + 1) * ff_chunk, :])
    if final:
        acc = _rms(acc) * fw_ref[...]
    o_ref[...] = acc


def _mlp(x2, norm_w, w1, w2, final_w, *, final, tb=1024, ff_chunk=1024):
    n, d = x2.shape
    dff = w1.shape[1]
    return pl.pallas_call(
        functools.partial(_mlp_body, ff_chunk=ff_chunk, final=final),
        grid=(n // tb,),
        in_specs=[
            pl.BlockSpec((tb, d), lambda i: (i, 0)),
            _resident((1, d)),
            _resident((d, dff)),
            _resident((dff, d)),
            _resident((1, d)),
        ],
        out_specs=pl.BlockSpec((tb, d), lambda i: (i, 0)),
        out_shape=jax.ShapeDtypeStruct((n, d), F32),
        compiler_params=_params(1),
        name="sq_relu_mlp",
    )(x2, norm_w.reshape(1, d), w1.astype(BF16), w2.astype(BF16), final_w.reshape(1, d))


def _attn_body(x_ref, nw_ref, wq_ref, k_ref, v_ref, wo_ref, o_ref, *, heads):
    x = x_ref[0]
    d = x.shape[-1]
    hd = d // heads
    hb = (_rms(x) * nw_ref[...]).astype(BF16)
    q = (_dot(hb, wq_ref[...]) * (hd ** -0.5)).astype(BF16)
    outs = []
    for h in range(heads):
        sl = slice(h * hd, (h + 1) * hd)
        s = _dot_nt(q[:, sl], k_ref[0, 0, :, sl])
        p = jnp.exp(s - jnp.max(s, axis=-1, keepdims=True))
        den = jnp.sum(p, axis=-1, keepdims=True)
        oh = _dot(p.astype(BF16), v_ref[0, 0, :, sl]) * (1.0 / den)
        outs.append(oh.astype(BF16))
    o = jnp.concatenate(outs, axis=-1)
    o_ref[0] = x + _dot(o, wo_ref[...])


def _cross_attention(x, norm_w, wq, k_all, v_all, wo, layer, *, tb=1024):
    b, t, d = x.shape
    m = k_all.shape[2]
    return pl.pallas_call(
        functools.partial(_attn_body, heads=XATTN_HEADS),
        grid=(b, t // tb),
        in_specs=[
            pl.BlockSpec((1, tb, d), lambda i, j: (i, j, 0)),
            _resident((1, d)),
            _resident((d, d)),
            pl.BlockSpec((1, 1, m, d), lambda i, j: (layer, i, 0, 0)),
            pl.BlockSpec((1, 1, m, d), lambda i, j: (layer, i, 0, 0)),
            _resident((d, d)),
        ],
        out_specs=pl.BlockSpec((1, tb, d), lambda i, j: (i, j, 0)),
        out_shape=jax.ShapeDtypeStruct((b, t, d), F32),
        compiler_params=_params(2),
        name="memory_xattn",
    )(x, norm_w.reshape(1, d), wq.astype(BF16), k_all, v_all, wo.astype(BF16))


def _slab_conv(u_ref, c_ref, w_taps, bias, *, hist, n_t, t_step, t_lo=None):
    taps = len(w_taps)
    n_slab = bias.shape[0]
    p_u = u_ref.shape[0] // n_slab
    p_c = c_ref.shape[0] // n_slab

    def step(i, carry):
        t0 = i * t_step if isinstance(i, int) else pl.multiple_of(i * t_step, t_step)
        accs = [bias] * t_step
        for k in range(t_step + taps - 1):
            uk = u_ref[pl.ds(t0 + hist - (taps - 1) + k, n_slab, stride=p_u), :]
            for tt in range(t_step):
                if 0 <= k - tt < taps:
                    accs[tt] = accs[tt] + w_taps[k - tt] * uk
        for tt in range(t_step):
            c_ref[pl.ds(t0 + tt, n_slab, stride=p_c), :] = accs[tt]
        return carry

    if t_lo is None:
        lax.fori_loop(0, n_t // t_step, step, 0)
    else:
        for i in range(t_lo // t_step, (t_lo + n_t) // t_step):
            step(i, 0)


def _slab_pitches(hist, n_t):
    return hist + n_t + 4, n_t + SUBLANES


def _conf_body(x_ref, nw_ref, w1_ref, b1_ref, wdw_ref, bdw_ref, lnw_ref, lnb_ref, w2_ref, b2_ref,
               o_ref, u_ref, c_ref, *, taps, hist, t_step):
    tb, d = x_ref.shape[1], x_ref.shape[2]
    n_slab = d // LANES
    p_u = u_ref.shape[0] // n_slab
    p_c = c_ref.shape[0] // n_slab

    @pl.when(pl.program_id(1) == 0)
    def _():
        for s in range(n_slab):
            u_ref[s * p_u:s * p_u + hist, :] = jnp.zeros((hist, LANES), F32)

    half = tb // 2
    w_taps = [wdw_ref[j] for j in range(taps)]

    def glu_to_slabs(ag, lo):
        u = ag[:, :d] * _sigmoid(ag[:, d:])
        for s in range(n_slab):
            u_ref[s * p_u + hist + lo:s * p_u + hist + lo + half, :] = u[:, s * LANES:(s + 1) * LANES]

    def finish(xh, lo):
        c = jnp.concatenate([c_ref[s * p_c + lo:s * p_c + lo + half, :] for s in range(n_slab)], axis=-1)
        mu = jnp.mean(c, axis=-1, keepdims=True)
        cc = c - mu
        var = jnp.mean(cc * cc, axis=-1, keepdims=True)
        yn = cc * lax.rsqrt(var + EPS) * lnw_ref[...] + lnb_ref[...]
        ys = _silu(yn).astype(BF16)
        o_ref[0, lo:lo + half, :] = xh + _dot(ys, w2_ref[...]) + b2_ref[...]

    x0 = x_ref[0, 0:half, :]
    x1 = x_ref[0, half:tb, :]
    ag0 = _dot((_rms(x0) * nw_ref[...]).astype(BF16), w1_ref[...]) + b1_ref[...]
    glu_to_slabs(ag0, 0)
    ag1 = _dot((_rms(x1) * nw_ref[...]).astype(BF16), w1_ref[...]) + b1_ref[...]
    _slab_conv(u_ref, c_ref, w_taps, bdw_ref[...], hist=hist, n_t=half, t_step=t_step, t_lo=0)
    glu_to_slabs(ag1, half)
    finish(x0, 0)
    _slab_conv(u_ref, c_ref, w_taps, bdw_ref[...], hist=hist, n_t=half, t_step=t_step, t_lo=half)
    for s in range(n_slab):
        u_ref[s * p_u:s * p_u + hist, :] = u_ref[s * p_u + tb:s * p_u + tb + hist, :]
    finish(x1, half)


def _conformer(x, norm_w, w_pw1, b_pw1, w_dw, b_dw, ln_w, ln_b, w_pw2, b_pw2, *, tb=1024, t_step=16):
    b, t, d = x.shape
    taps = w_dw.shape[0]
    n_slab = d // LANES
    assert n_slab == SUBLANES
    hist = -(-(taps - 1) // SUBLANES) * SUBLANES
    p_u, p_c = _slab_pitches(hist, tb)
    return pl.pallas_call(
        functools.partial(_conf_body, taps=taps, hist=hist, t_step=t_step),
        grid=(b, t // tb),
        in_specs=[
            pl.BlockSpec((1, tb, d), lambda i, j: (i, j, 0)),
            _resident((1, d)),
            _resident((d, 2 * d)),
            _resident((1, 2 * d)),
            _resident((taps, n_slab, LANES)),
            _resident((n_slab, LANES)),
            _resident((1, d)),
            _resident((1, d)),
            _resident((d, d)),
            _resident((1, d)),
        ],
        out_specs=pl.BlockSpec((1, tb, d), lambda i, j: (i, j, 0)),
        out_shape=jax.ShapeDtypeStruct((b, t, d), F32),
        scratch_shapes=[pltpu.VMEM((n_slab * p_u, LANES), F32), pltpu.VMEM((n_slab * p_c, LANES), F32)],
        compiler_params=_params(2),
        name="conformer_conv",
    )(x, norm_w.reshape(1, d), w_pw1.astype(BF16), b_pw1.reshape(1, 2 * d), w_dw.reshape(taps, n_slab, LANES),
      b_dw.reshape(n_slab, LANES), ln_w.reshape(1, d), ln_b.reshape(1, d), w_pw2.astype(BF16), b_pw2.reshape(1, d))


def _iota2(shape, axis):
    return lax.broadcasted_iota(jnp.int32, shape, axis)


HGRN_GROUP = SUB * SUBLANES
BLOCK_PITCH = SUB + 4
GROUP_PITCH = SUBLANES * BLOCK_PITCH
LOG2E = 1.4426950408889634
LOCAL_UNROLL = 8


def _mix_body(x_ref, nw_ref, win_ref, lbl_ref, onw_ref, cw_ref, cb_ref, dtb_ref, alog_ref, dsk_ref,
              snw_ref, wout_ref, o_ref,
              pr_ref, qs_ref, ks_ref, bs_ref, vs_ref, od_ref, qd_ref, kd_ref, bd_ref, qx_ref, kx_ref,
              su_ref, sc_ref, xbc_ref, xc_ref, xe_ref, ecs_ref, cs_ref, csr_ref, yd_ref, oc_ref, sth_ref, sts_ref, y_ref, *, e, t_step):
    tb, d = x_ref.shape[1], x_ref.shape[2]
    hk = HGRN_HEADS * HGRN_DK
    sd = SSD_HEADS * SSD_HEADDIM
    gn = SSD_GROUPS * SSD_STATE
    c_q, c_f, c_i, c_g = 0, hk, 2 * hk, 3 * hk
    c_z = 4 * hk
    c_x = c_z + sd
    c_dt = c_x + sd + 2 * gn
    hp = SSD_HEADS // SSD_GROUPS * SSD_HEADDIM
    taps = cw_ref.shape[0]
    hist = PAD_ROWS
    n_slab = cb_ref.shape[0]
    p_u = su_ref.shape[0] // n_slab
    p_c = sc_ref.shape[0] // n_slab
    n_groups = tb // HGRN_GROUP
    n_chunks = tb // CHUNK

    @pl.when(pl.program_id(1) == 0)
    def _():
        for s in range(n_slab):
            su_ref[s * p_u:s * p_u + hist, :] = jnp.zeros((hist, LANES), F32)
        sth_ref[...] = jnp.zeros(sth_ref.shape, F32)
        sts_ref[...] = jnp.zeros(sts_ref.shape, F32)

    x = x_ref[0]
    hb = (_rms(x) * nw_ref[...]).astype(BF16)
    pr_ref[:, 0:c_g] = _dot(hb, win_ref[:, 0:c_g])

    lg = lbl_ref[...]
    ex = jnp.exp(lg - jnp.max(lg, axis=0, keepdims=True))
    probs = ex / jnp.sum(ex, axis=0, keepdims=True)
    lb = jnp.sum(probs[0:e + 1], axis=0, keepdims=True) - probs[0:1]

    rg = _iota2((HGRN_GROUP, HGRN_GROUP), 0)
    cg = _iota2((HGRN_GROUP, HGRN_GROUP), 1)
    same_chunk = (rg // CHUNK) == (cg // CHUNK)
    m01 = lambda cond: jnp.where(same_chunk & cond, 1.0, 0.0).astype(BF16)
    tri_incl = m01(cg <= rg)
    r64 = _iota2((CHUNK, CHUNK), 0)
    c64 = _iota2((CHUNK, CHUNK), 1)
    tri64 = jnp.where(c64 <= r64, 1.0, 0.0).astype(BF16)
    lane2 = _iota2((CHUNK, LANES), 1)
    causal2 = (lane2 % CHUNK) <= _iota2((CHUNK, LANES), 0)
    rb = _iota2((tb, tb), 0)
    cb_ = _iota2((tb, tb), 1)
    tri_rows = jnp.where(((rb // CHUNK) == (cb_ // CHUNK)) & (rb <= cb_), 1.0, 0.0).astype(BF16)
    n_pair = SSD_HEADS // 2
    er = _iota2((LANES, sd), 0)
    ehead = jnp.where(er < n_pair, 2 * er, 2 * (er - n_pair) + 1)
    expand = jnp.where((_iota2((LANES, sd), 1) // SSD_HEADDIM == ehead) & (er < SSD_HEADS), 1.0, 0.0).astype(BF16)
    ones_k = jnp.ones((HGRN_DK, HGRN_DK), BF16)
    pair_diag = (_iota2((LANES, LANES), 0) // SSD_HEADDIM) == (_iota2((LANES, LANES), 1) // SSD_HEADDIM)

    for g in range(n_groups):
        rows = slice(g * HGRN_GROUP, (g + 1) * HGRN_GROUP)
        orow = rows
        fr = pr_ref[rows, c_f:c_f + hk]
        lf = jnp.log(lb + (1.0 - lb) * _sigmoid(fr)) * LOG2E
        kk = (1.0 - lb) * _sigmoid(-fr)
        bcs = _dot01(tri_incl, lf)
        nb = CHUNK // SUB
        n_blk = HGRN_GROUP // SUB
        ends = [bcs[(i + 1) * SUB - 1:(i + 1) * SUB, :] for i in range(n_blk)]
        zero_row = jnp.zeros((1, hk), F32)

        def at_block_end(back):
            rows_ = [ends[i - back] if (i % nb) - back >= 0 else zero_row for i in range(n_blk)]
            return jnp.concatenate([jnp.broadcast_to(r_, (SUB, hk)) for r_ in rows_], axis=0)

        btot = jnp.concatenate([jnp.broadcast_to(ends[(i // nb) * nb + nb - 1], (SUB, hk)) for i in range(n_blk)], axis=0)
        qq = pr_ref[rows, c_q:c_q + hk] * (HGRN_DK ** -0.5)
        vv = pr_ref[rows, c_i:c_i + hk]
        qd_ref[orow, :] = (qq * jnp.exp2(bcs)).astype(BF16)
        kd_ref[orow, :] = (kk * jnp.exp2(btot - bcs)).astype(BF16)
        bd_ref[orow, :] = jnp.exp2(btot)
        kx_ref[orow, :] = (kk * jnp.exp2(at_block_end(0) - bcs)).astype(BF16)
        for m in range(1, nb):
            qx_ref[m - 1, orow, :] = (qq * jnp.exp2(bcs - at_block_end(m))).astype(BF16)
        for h in range(HGRN_HEADS):
            hs = slice(h * HGRN_DK, (h + 1) * HGRN_DK)
            for blk in range(SUBLANES):
                src = slice(blk * SUB, (blk + 1) * SUB)
                lo = (h * n_groups + g) * GROUP_PITCH + blk * BLOCK_PITCH
                qs_ref[lo:lo + SUB, :] = qq[src, hs]
                ks_ref[lo:lo + SUB, :] = kk[src, hs]
                bs_ref[lo:lo + SUB, :] = bcs[src, hs]
                vs_ref[lo:lo + SUB, :] = vv[src, hs]

    pr_ref[:, c_g:] = _dot(hb, win_ref[:, c_g:])

    def pair_step(idx, carry):
        sb = pl.multiple_of(idx * GROUP_PITCH, SUBLANES)
        ob = pl.multiple_of(idx * HGRN_GROUP, HGRN_GROUP)
        ld = lambda ref, r: ref[pl.ds(sb + r, SUBLANES, stride=BLOCK_PITCH), :]
        qv = [ld(qs_ref, r) for r in range(SUB)]
        kv = [ld(ks_ref, r) for r in range(SUB)]
        bv = [ld(bs_ref, r) for r in range(SUB)]
        vv_ = [ld(vs_ref, r) for r in range(SUB)]
        prods = []
        for r in range(SUB):
            for s_ in range(r + 1):
                if s_ == r:
                    prods.append(qv[r] * kv[r])
                else:
                    prods.append(qv[r] * kv[s_] * jnp.exp2(bv[r] - bv[s_]))
        sc = _dot(jnp.concatenate(prods, axis=0).astype(BF16), ones_k)
        n = 0
        for r in range(SUB):
            acc = None
            for s_ in range(r + 1):
                term = sc[n * SUBLANES:(n + 1) * SUBLANES, :] * vv_[s_]
                acc = term if acc is None else acc + term
                n += 1
            od_ref[pl.ds(ob + r, SUBLANES, stride=SUB), :] = acc
        return carry

    lax.fori_loop(0, HGRN_HEADS * n_groups, pair_step, 0, unroll=2)

    for s in range(n_slab):
        su_ref[s * p_u + hist:s * p_u + hist + tb, :] = pr_ref[:, c_x + s * LANES:c_x + (s + 1) * LANES]
    _slab_conv(su_ref, sc_ref, [cw_ref[j] for j in range(taps)], cb_ref[...], hist=hist, n_t=tb, t_step=t_step)
    for s in range(n_slab):
        su_ref[s * p_u:s * p_u + hist, :] = su_ref[s * p_u + tb:s * p_u + tb + hist, :]
    xbc_ref[...] = _silu(jnp.concatenate([sc_ref[s * p_c:s * p_c + tb, :] for s in range(n_slab)], axis=-1))
    dt = jnp.logaddexp(pr_ref[:, c_dt:c_dt + LANES] + dtb_ref[...], 0.0)
    da = dt * (-jnp.exp(alog_ref[...]))
    d0, d1, d2 = _split3(da)
    for c in range(n_chunks):
        cr = slice(c * CHUNK, (c + 1) * CHUNK)
        cs_ref[cr, :] = _dot(tri64, d0[cr]) + _dot(tri64, d1[cr]) + _dot(tri64, d2[cr])
    cs = cs_ref[...]
    cs_end = jnp.concatenate(
        [jnp.broadcast_to(cs[(c + 1) * CHUNK - 1:(c + 1) * CHUNK, :], (CHUNK, LANES)) for c in range(n_chunks)], axis=0)
    cs_t = _dot_tn(d0, tri_rows) + _dot_tn(d1, tri_rows) + _dot_tn(d2, tri_rows)
    for c in range(n_chunks):
        cl = slice(c * CHUNK, (c + 1) * CHUNK)
        csr_ref[c] = jnp.concatenate([cs_t[0:SUBLANES, cl], cs_t[n_pair:n_pair + SUBLANES, cl]], axis=1)
    e0, e1, e2 = _split3(jnp.exp(cs))
    spread = _dot(jnp.concatenate([dt.astype(BF16), jnp.exp(cs_end - cs).astype(BF16), e0, e1, e2], axis=0), expand)
    xc = xbc_ref[:, 0:sd] * spread[0:tb]
    xc_ref[...] = xc.astype(BF16)
    xe_ref[...] = (xc * spread[tb:2 * tb]).astype(BF16)
    ecs_ref[...] = spread[2 * tb:3 * tb] + spread[3 * tb:4 * tb] + spread[4 * tb:5 * tb]

    def local_step(it, carry):
        nb = CHUNK // SUB
        lane64 = _iota2((CHUNK, CHUNK), 1)
        chunks = [it * LOCAL_UNROLL + u for u in range(LOCAL_UNROLL)]
        r0s = [pl.multiple_of(c * CHUNK, CHUNK) for c in chunks]
        stage1 = []
        for c, r0 in zip(chunks, r0s):
            rows = pl.ds(r0, CHUNK)
            res = []
            for h in range(HGRN_HEADS):
                hs = slice(h * HGRN_DK, (h + 1) * HGRN_DK)
                lhs = jnp.concatenate(
                    [qx_ref[i - j - 1, pl.ds(r0 + i * SUB, SUB), hs] for j in range(nb - 1) for i in range(j + 1, nb)],
                    axis=0)
                res.append(_dot_nt(lhs, kx_ref[rows, hs]))
            cbm2 = []
            for g in range(SSD_GROUPS):
                bm = xbc_ref[rows, sd + g * SSD_STATE:sd + (g + 1) * SSD_STATE].astype(BF16)
                cm = xbc_ref[rows, sd + gn + g * SSD_STATE:sd + gn + (g + 1) * SSD_STATE].astype(BF16)
                cbm2.append(_dot_nt(cm, jnp.concatenate([bm, bm], axis=0)))
            stage1.append((res, cbm2))
        for (c, r0), (res, cbm2) in zip(zip(chunks, r0s), stage1):
            rows = pl.ds(r0, CHUNK)
            for h in range(HGRN_HEADS):
                vi = pr_ref[rows, c_i + h * HGRN_DK:c_i + (h + 1) * HGRN_DK].astype(BF16)
                sc = jnp.zeros((CHUNK, CHUNK), F32)
                off = 0
                for j in range(nb - 1):
                    n_r = (nb - 1 - j) * SUB
                    col = jnp.concatenate([jnp.zeros((CHUNK - n_r, CHUNK), F32), res[h][off:off + n_r, :]], axis=0)
                    sc = jnp.where((lane64 // SUB) == j, col, sc)
                    off += n_r
                oc_ref[pl.ds(h * tb + r0, CHUNK), :] = _dot(sc.astype(BF16), vi)
            cs_c = cs_ref[rows, :]
            cs_rows = csr_ref[c]
            for p in range(n_pair):
                cs_pair = jnp.where(lane2 < CHUNK, cs_c[:, p:p + 1], cs_c[:, n_pair + p:n_pair + p + 1])
                dec = jnp.where(causal2, jnp.exp(cs_pair - cs_rows[p:p + 1, :]), 0.0)
                xp = xc_ref[rows, p * LANES:(p + 1) * LANES]
                rhs = jnp.where(pair_diag, jnp.concatenate([xp, xp], axis=0), jnp.zeros((LANES, LANES), BF16))
                g = p // (n_pair // SSD_GROUPS)
                yd_ref[rows, p * LANES:(p + 1) * LANES] = _dot((cbm2[g] * dec).astype(BF16), rhs)
        return carry

    lax.fori_loop(0, n_chunks // LOCAL_UNROLL, local_step, 0)

    def state_step(c, carry):
        r0 = pl.multiple_of(c * CHUNK, CHUNK)
        rows = pl.ds(r0, CHUNK)
        gate = _silu(pr_ref[rows, c_g:c_g + hk])
        for h in range(HGRN_HEADS):
            hs = slice(h * HGRN_DK, (h + 1) * HGRN_DK)
            vi = pr_ref[rows, c_i + h * HGRN_DK:c_i + (h + 1) * HGRN_DK].astype(BF16)
            st = sth_ref[h]
            orow = pl.ds(h * tb + r0, CHUNK)
            o_h = od_ref[orow, :] + oc_ref[orow, :] + _dot_nt(qd_ref[rows, hs], st.astype(BF16))
            sth_ref[h] = st * bd_ref[pl.ds(r0, 1), hs] + _dot_tn(vi, kd_ref[rows, hs])
            o_h = _rms(o_h) * onw_ref[:, hs] * gate[:, hs]
            y_ref[rows, hs] = o_h.astype(BF16)

        xbc = xbc_ref[rows, :]
        ecs = ecs_ref[rows, :]
        xcd = xe_ref[rows, :]
        ys = []
        for g in range(SSD_GROUPS):
            bm = xbc[:, sd + g * SSD_STATE:sd + (g + 1) * SSD_STATE].astype(BF16)
            cm = xbc[:, sd + gn + g * SSD_STATE:sd + gn + (g + 1) * SSD_STATE].astype(BF16)
            gl = slice(g * hp, (g + 1) * hp)
            st = sts_ref[g]
            ys.append(_dot(cm, st.astype(BF16)) * ecs[:, gl])
            sts_ref[g] = st * ecs[CHUNK - 1:CHUNK, gl] + _dot_tn(bm, xcd[:, gl])
        y = jnp.concatenate(ys, axis=-1) + yd_ref[rows, :] + xbc[:, 0:sd] * dsk_ref[...]
        y = y * _silu(pr_ref[rows, c_z:c_z + sd])
        y = jnp.concatenate([_rms(y[:, g * hp:(g + 1) * hp]) for g in range(SSD_GROUPS)], axis=-1) * snw_ref[...]
        y_ref[rows, hk:hk + sd] = y.astype(BF16)
        return carry

    lax.fori_loop(0, n_chunks, state_step, 0, unroll=4)

    o_ref[0] = x + _dot(y_ref[...], wout_ref[...])


def _mixer(x, norm_w, w_in, lb_logits, out_norm_w, conv_w, conv_b, dt_bias, a_log, d_skip, ssd_norm_w,
           w_out, e, *, tb=512, t_step=16):
    b, t, d = x.shape
    n_even, hk = lb_logits.shape
    sd = SSD_HEADS * SSD_HEADDIM
    ab_in = w_in.shape[1]
    pw = -(-ab_in // LANES) * LANES
    n_dt = SSD_HEADS
    order = list(range(0, n_dt, 2)) + list(range(1, n_dt, 2))
    w_dt = w_in[:, ab_in - n_dt:][:, order]
    w_in_p = jnp.pad(jnp.concatenate([w_in[:, :ab_in - n_dt], w_dt], axis=1), ((0, 0), (0, pw - ab_in))).astype(BF16)
    pad8 = lambda a: jnp.pad(a[jnp.array(order)].reshape(1, -1), ((0, 0), (0, LANES - a.shape[-1])))
    taps, cc = conv_w.shape
    n_slab = cc // LANES
    assert tb % HGRN_GROUP == 0 and n_slab == SUBLANES and taps - 1 <= PAD_ROWS
    p_u, p_c = _slab_pitches(PAD_ROWS, tb)
    slab_rows = HGRN_HEADS * (tb // HGRN_GROUP) * GROUP_PITCH
    slab = pltpu.VMEM((slab_rows, LANES), F32)
    return pl.pallas_call(
        functools.partial(_mix_body, e=e, t_step=t_step),
        grid=(b, t // tb),
        in_specs=[
            pl.BlockSpec((1, tb, d), lambda i, j: (i, j, 0)),
            _resident((1, d)),
            _resident((d, pw)),
            _resident((n_even, hk)),
            _resident((1, hk)),
            _resident((taps, n_slab, LANES)),
            _resident((n_slab, LANES)),
            _resident((1, LANES)),
            _resident((1, LANES)),
            _resident((1, sd)),
            _resident((1, sd)),
            _resident((hk + sd, d)),
        ],
        out_specs=pl.BlockSpec((1, tb, d), lambda i, j: (i, j, 0)),
        out_shape=jax.ShapeDtypeStruct((b, t, d), F32),
        scratch_shapes=[
            pltpu.VMEM((tb, pw), F32),
            slab, slab, slab, slab,
            pltpu.VMEM((HGRN_HEADS * tb, LANES), F32),
            pltpu.VMEM((tb, hk), BF16),
            pltpu.VMEM((tb, hk), BF16),
            pltpu.VMEM((tb, hk), F32),
            pltpu.VMEM((CHUNK // SUB - 1, tb, hk), BF16),
            pltpu.VMEM((tb, hk), BF16),
            pltpu.VMEM((n_slab * p_u, LANES), F32),
            pltpu.VMEM((n_slab * p_c, LANES), F32),
            pltpu.VMEM((tb, cc), F32),
            pltpu.VMEM((tb, sd), BF16),
            pltpu.VMEM((tb, sd), BF16),
            pltpu.VMEM((tb, sd), F32),
            pltpu.VMEM((tb, LANES), F32),
            pltpu.VMEM((tb // CHUNK, SUBLANES, LANES), F32),
            pltpu.VMEM((tb, sd), F32),
            pltpu.VMEM((HGRN_HEADS * tb, LANES), F32),
            pltpu.VMEM((HGRN_HEADS, HGRN_DK, HGRN_DK), F32),
            pltpu.VMEM((SSD_GROUPS, SSD_STATE, sd // SSD_GROUPS), F32),
            pltpu.VMEM((tb, hk + sd), BF16),
        ],
        compiler_params=_params(2),
        name="hgrn_ssd_mixer",
    )(x, norm_w.reshape(1, d), w_in_p, lb_logits, out_norm_w.reshape(1, hk), conv_w.reshape(taps, n_slab, LANES),
      conv_b.reshape(n_slab, LANES),
      pad8(dt_bias), pad8(a_log), jnp.repeat(d_skip, SSD_HEADDIM).reshape(1, sd), ssd_norm_w.reshape(1, sd),
      w_out.astype(BF16))


def kernel(x, mem, mem_norm_w, norm_mix_w, ab_w_in, hgrn_lb_logits, hgrn_out_norm_w, ssd_conv_w, ssd_conv_b,
           ssd_dt_bias, ssd_a_log, ssd_d, ssd_norm_w, ab_w_out, cv_w_pw1, cv_b_pw1, cv_w_dw, cv_b_dw, cv_ln_w,
           cv_ln_b, cv_w_pw2, cv_b_pw2, norm_xattn_w, xattn_wq, xattn_wk, xattn_wv, xattn_wo, norm_mlp_w,
           mlp_w1, mlp_w2, final_norm_w):
    b, t, d = x.shape
    depth = norm_mix_w.shape[0]
    k_all, v_all = _memory_kv(mem, mem_norm_w, xattn_wk, xattn_wv)
    for layer in range(depth):
        if layer % 2 == 0:
            e = layer // 2
            x = _mixer(x, norm_mix_w[layer], ab_w_in[e], hgrn_lb_logits, hgrn_out_norm_w[e], ssd_conv_w[e],
                       ssd_conv_b[e], ssd_dt_bias[e], ssd_a_log[e], ssd_d[e], ssd_norm_w[e], ab_w_out[e], e)
        else:
            o = layer // 2
            x = _conformer(x, norm_mix_w[layer], cv_w_pw1[o], cv_b_pw1[o], cv_w_dw[o], cv_b_dw[o], cv_ln_w[o],
                           cv_ln_b[o], cv_w_pw2[o], cv_b_pw2[o])
        x = _cross_attention(x, norm_xattn_w[layer], xattn_wq[layer], k_all, v_all, xattn_wo[layer], layer)
        x = _mlp(x.reshape(b * t, d), norm_mlp_w[layer], mlp_w1[layer], mlp_w2[layer], final_norm_w,
                 final=(layer == depth - 1)).reshape(b, t, d)
    return x
```

```python
import functools

import jax
import jax.numpy as jnp
from jax import lax
from jax.experimental import pallas as pl
from jax.experimental.pallas import tpu as pltpu

F32 = jnp.float32
BF16 = jnp.bfloat16

EPS = 1e-6
CHUNK = 64
SUB = 16
LANES = 128
SUBLANES = 8
PAD_ROWS = 8

HGRN_HEADS = 4
HGRN_DK = 128
SSD_HEADS = 8
SSD_HEADDIM = 64
SSD_GROUPS = 2
SSD_STATE = 128
XATTN_HEADS = 4

VMEM_LIMIT_BYTES = 56 * 1024 * 1024


def _rms(x):
    return x * lax.rsqrt(jnp.mean(x * x, axis=-1, keepdims=True) + EPS)


def _sigmoid(x):
    return 1.0 / (1.0 + jnp.exp(-x))


def _silu(x):
    return x * _sigmoid(x)


def _dot(a, b):
    return jnp.dot(a, b, preferred_element_type=F32)


def _dot_nt(a, b):
    return lax.dot_general(a, b, (((1,), (1,)), ((), ())), preferred_element_type=F32)


def _dot_tn(a, b):
    return lax.dot_general(a, b, (((0,), (0,)), ((), ())), preferred_element_type=F32)


def _split3(a):
    a0 = a.astype(BF16)
    r1 = a - a0.astype(F32)
    a1 = r1.astype(BF16)
    a2 = (r1 - a1.astype(F32)).astype(BF16)
    return a0, a1, a2


def _dot01(m01, a):
    a0, a1, a2 = _split3(a)
    return _dot(m01, a0) + _dot(m01, a1) + _dot(m01, a2)


def _resident(shape):
    nd = len(shape)
    return pl.BlockSpec(shape, lambda *_: (0,) * nd, pipeline_mode=pl.Buffered(1))


def _layer(shape, layer):
    nd = len(shape)
    return pl.BlockSpec((None,) + tuple(shape), lambda *_: (layer,) + (0,) * nd, pipeline_mode=pl.Buffered(1))


def _rows3(a):
    return a.reshape(a.shape[0], 1, a.shape[-1])


def _params(n_axes):
    return pltpu.CompilerParams(dimension_semantics=("arbitrary",) * n_axes,
                                vmem_limit_bytes=VMEM_LIMIT_BYTES)


def _kv_body(mem_ref, mw_ref, wk_ref, wv_ref, k_ref, v_ref):
    m = mem_ref[0]
    mn = (_rms(m) * mw_ref[...]).astype(BF16)
    k_ref[0, 0] = _dot(mn, wk_ref[0].astype(BF16)).astype(BF16)
    v_ref[0, 0] = _dot(mn, wv_ref[0].astype(BF16)).astype(BF16)


def _memory_kv(mem, mem_norm_w, wk, wv):
    depth, d, _ = wk.shape
    b, m, _ = mem.shape
    out = jax.ShapeDtypeStruct((depth, b, m, d), BF16)
    return pl.pallas_call(
        _kv_body,
        grid=(depth, b),
        in_specs=[
            pl.BlockSpec((1, m, d), lambda l, i: (i, 0, 0)),
            pl.BlockSpec((1, d), lambda l, i: (0, 0)),
            pl.BlockSpec((1, d, d), lambda l, i: (l, 0, 0)),
            pl.BlockSpec((1, d, d), lambda l, i: (l, 0, 0)),
        ],
        out_specs=[
            pl.BlockSpec((1, 1, m, d), lambda l, i: (l, i, 0, 0)),
            pl.BlockSpec((1, 1, m, d), lambda l, i: (l, i, 0, 0)),
        ],
        out_shape=[out, out],
        compiler_params=_params(2),
        name="memory_kv",
    )(mem, mem_norm_w.reshape(1, d), wk, wv)


def _mlp_body(x_ref, nw_ref, w1_ref, w2_ref, fw_ref, o_ref, *, ff_chunk, final):
    x = x_ref[...]
    hb = (_rms(x) * nw_ref[...]).astype(BF16)
    acc = x
    for c in range(w1_ref.shape[1] // ff_chunk):
        a = _dot(hb, w1_ref[:, c * ff_chunk:(c + 1) * ff_chunk])
        a = jnp.square(jnp.maximum(a, 0.0)).astype(BF16)
        acc = acc + _dot(a, w2_ref[c * ff_chunk:(c + 1) * ff_chunk, :])
    if final:
        acc = _rms(acc) * fw_ref[...]
    o_ref[...] = acc


def _mlp(x2, norm_w, w1, w2, final_w, layer, *, final, tb=1024, ff_chunk=1024):
    n, d = x2.shape
    dff = w1.shape[2]
    return pl.pallas_call(
        functools.partial(_mlp_body, ff_chunk=ff_chunk, final=final),
        grid=(n // tb,),
        in_specs=[
            pl.BlockSpec((tb, d), lambda i: (i, 0)),
            _layer((1, d), layer),
            _layer((d, dff), layer),
            _layer((dff, d), layer),
            _resident((1, d)),
        ],
        out_specs=pl.BlockSpec((tb, d), lambda i: (i, 0)),
        out_shape=jax.ShapeDtypeStruct((n, d), F32),
        compiler_params=_params(1),
        name="sq_relu_mlp",
    )(x2, norm_w, w1, w2, final_w.reshape(1, d))


def _attn_body(x_ref, nw_ref, wq_ref, k_ref, v_ref, wo_ref, o_ref, *, heads):
    x = x_ref[0]
    d = x.shape[-1]
    hd = d // heads
    hb = (_rms(x) * nw_ref[...]).astype(BF16)
    q = (_dot(hb, wq_ref[...]) * (hd ** -0.5)).astype(BF16)
    outs = []
    for h in range(heads):
        sl = slice(h * hd, (h + 1) * hd)
        s = _dot_nt(q[:, sl], k_ref[0, 0, :, sl])
        p = jnp.exp(s - jnp.max(s, axis=-1, keepdims=True))
        den = jnp.sum(p, axis=-1, keepdims=True)
        oh = _dot(p.astype(BF16), v_ref[0, 0, :, sl]) * (1.0 / den)
        outs.append(oh.astype(BF16))
    o = jnp.concatenate(outs, axis=-1)
    o_ref[0] = x + _dot(o, wo_ref[...])


def _cross_attention(x, norm_w, wq, k_all, v_all, wo, layer, *, tb=1024):
    b, t, d = x.shape
    m = k_all.shape[2]
    return pl.pallas_call(
        functools.partial(_attn_body, heads=XATTN_HEADS),
        grid=(b, t // tb),
        in_specs=[
            pl.BlockSpec((1, tb, d), lambda i, j: (i, j, 0)),
            _layer((1, d), layer),
            _layer((d, d), layer),
            pl.BlockSpec((1, 1, m, d), lambda i, j: (layer, i, 0, 0)),
            pl.BlockSpec((1, 1, m, d), lambda i, j: (layer, i, 0, 0)),
            _layer((d, d), layer),
        ],
        out_specs=pl.BlockSpec((1, tb, d), lambda i, j: (i, j, 0)),
        out_shape=jax.ShapeDtypeStruct((b, t, d), F32),
        compiler_params=_params(2),
        name="memory_xattn",
    )(x, norm_w, wq, k_all, v_all, wo)


def _slab_conv(u_ref, c_ref, w_taps, bias, *, hist, n_t, t_step, t_lo=None):
    taps = len(w_taps)
    n_slab = bias.shape[0]
    p_u = u_ref.shape[0] // n_slab
    p_c = c_ref.shape[0] // n_slab

    def step(i, carry):
        t0 = i * t_step if isinstance(i, int) else pl.multiple_of(i * t_step, t_step)
        accs = [bias] * t_step
        for k in range(t_step + taps - 1):
            uk = u_ref[pl.ds(t0 + hist - (taps - 1) + k, n_slab, stride=p_u), :]
            for tt in range(t_step):
                if 0 <= k - tt < taps:
                    accs[tt] = accs[tt] + w_taps[k - tt] * uk
        for tt in range(t_step):
            c_ref[pl.ds(t0 + tt, n_slab, stride=p_c), :] = accs[tt]
        return carry

    if t_lo is None:
        lax.fori_loop(0, n_t // t_step, step, 0)
    else:
        for i in range(t_lo // t_step, (t_lo + n_t) // t_step):
            step(i, 0)


def _slab_pitches(hist, n_t):
    return hist + n_t + 4, n_t + SUBLANES


def _conf_body(x_ref, nw_ref, w1_ref, b1_ref, wdw_ref, bdw_ref, lnw_ref, lnb_ref, w2_ref, b2_ref,
               o_ref, u_ref, c_ref, *, taps, hist, t_step):
    tb, d = x_ref.shape[1], x_ref.shape[2]
    n_slab = d // LANES
    p_u = u_ref.shape[0] // n_slab
    p_c = c_ref.shape[0] // n_slab

    @pl.when(pl.program_id(1) == 0)
    def _():
        for s in range(n_slab):
            u_ref[s * p_u:s * p_u + hist, :] = jnp.zeros((hist, LANES), F32)

    half = tb // 2
    w_taps = [wdw_ref[j] for j in range(taps)]

    def glu_to_slabs(ag, lo):
        u = ag[:, :d] * _sigmoid(ag[:, d:])
        for s in range(n_slab):
            u_ref[s * p_u + hist + lo:s * p_u + hist + lo + half, :] = u[:, s * LANES:(s + 1) * LANES]

    def finish(xh, lo):
        c = jnp.concatenate([c_ref[s * p_c + lo:s * p_c + lo + half, :] for s in range(n_slab)], axis=-1)
        mu = jnp.mean(c, axis=-1, keepdims=True)
        cc = c - mu
        var = jnp.mean(cc * cc, axis=-1, keepdims=True)
        yn = cc * lax.rsqrt(var + EPS) * lnw_ref[...] + lnb_ref[...]
        ys = _silu(yn).astype(BF16)
        o_ref[0, lo:lo + half, :] = xh + _dot(ys, w2_ref[...]) + b2_ref[...]

    x0 = x_ref[0, 0:half, :]
    x1 = x_ref[0, half:tb, :]
    ag0 = _dot((_rms(x0) * nw_ref[...]).astype(BF16), w1_ref[...]) + b1_ref[...]
    glu_to_slabs(ag0, 0)
    ag1 = _dot((_rms(x1) * nw_ref[...]).astype(BF16), w1_ref[...]) + b1_ref[...]
    _slab_conv(u_ref, c_ref, w_taps, bdw_ref[...], hist=hist, n_t=half, t_step=t_step, t_lo=0)
    glu_to_slabs(ag1, half)
    finish(x0, 0)
    _slab_conv(u_ref, c_ref, w_taps, bdw_ref[...], hist=hist, n_t=half, t_step=t_step, t_lo=half)
    for s in range(n_slab):
        u_ref[s * p_u:s * p_u + hist, :] = u_ref[s * p_u + tb:s * p_u + tb + hist, :]
    finish(x1, half)


def _conformer(x, norm_w, w_pw1, b_pw1, w_dw, b_dw, ln_w, ln_b, w_pw2, b_pw2, layer, o, *, tb=1024, t_step=16):
    b, t, d = x.shape
    taps = w_dw.shape[1]
    n_slab = d // LANES
    assert n_slab == SUBLANES
    hist = -(-(taps - 1) // SUBLANES) * SUBLANES
    p_u, p_c = _slab_pitches(hist, tb)
    return pl.pallas_call(
        functools.partial(_conf_body, taps=taps, hist=hist, t_step=t_step),
        grid=(b, t // tb),
        in_specs=[
            pl.BlockSpec((1, tb, d), lambda i, j: (i, j, 0)),
            _layer((1, d), layer),
            _layer((d, 2 * d), o),
            _layer((1, 2 * d), o),
            _layer((taps, n_slab, LANES), o),
            _layer((n_slab, LANES), o),
            _layer((1, d), o),
            _layer((1, d), o),
            _layer((d, d), o),
            _layer((1, d), o),
        ],
        out_specs=pl.BlockSpec((1, tb, d), lambda i, j: (i, j, 0)),
        out_shape=jax.ShapeDtypeStruct((b, t, d), F32),
        scratch_shapes=[pltpu.VMEM((n_slab * p_u, LANES), F32), pltpu.VMEM((n_slab * p_c, LANES), F32)],
        compiler_params=_params(2),
        name="conformer_conv",
    )(x, norm_w, w_pw1, _rows3(b_pw1), w_dw.reshape(-1, taps, n_slab, LANES), b_dw.reshape(-1, n_slab, LANES),
      _rows3(ln_w), _rows3(ln_b), w_pw2, _rows3(b_pw2))


def _iota2(shape, axis):
    return lax.broadcasted_iota(jnp.int32, shape, axis)


HGRN_GROUP = SUB * SUBLANES
BLOCK_PITCH = SUB + 4
GROUP_PITCH = SUBLANES * BLOCK_PITCH
LOG2E = 1.4426950408889634
LOCAL_UNROLL = 8


def _mix_body(x_ref, nw_ref, win_ref, lbl_ref, onw_ref, cw_ref, cb_ref, dtb_ref, alog_ref, dsk_ref,
              snw_ref, wout_ref, o_ref,
              pr_ref, qs_ref, ks_ref, bs_ref, vs_ref, od_ref, qd_ref, kd_ref, bd_ref, qx_ref, kx_ref,
              su_ref, sc_ref, xbc_ref, xc_ref, xe_ref, ecs_ref, cs_ref, csr_ref, yd_ref, oc_ref, sth_ref, sts_ref, y_ref, *, e, t_step):
    tb, d = x_ref.shape[1], x_ref.shape[2]
    hk = HGRN_HEADS * HGRN_DK
    sd = SSD_HEADS * SSD_HEADDIM
    gn = SSD_GROUPS * SSD_STATE
    c_q, c_f, c_i, c_g = 0, hk, 2 * hk, 3 * hk
    c_z = 4 * hk
    c_x = c_z + sd
    c_dt = c_x + sd + 2 * gn
    hp = SSD_HEADS // SSD_GROUPS * SSD_HEADDIM
    taps = cw_ref.shape[0]
    hist = PAD_ROWS
    n_slab = cb_ref.shape[0]
    p_u = su_ref.shape[0] // n_slab
    p_c = sc_ref.shape[0] // n_slab
    n_groups = tb // HGRN_GROUP
    n_chunks = tb // CHUNK

    @pl.when(pl.program_id(1) == 0)
    def _():
        for s in range(n_slab):
            su_ref[s * p_u:s * p_u + hist, :] = jnp.zeros((hist, LANES), F32)
        sth_ref[...] = jnp.zeros(sth_ref.shape, F32)
        sts_ref[...] = jnp.zeros(sts_ref.shape, F32)

    x = x_ref[0]
    hb = (_rms(x) * nw_ref[...]).astype(BF16)
    pr_ref[:, 0:c_g] = _dot(hb, win_ref[:, 0:c_g])

    lg = lbl_ref[...]
    ex = jnp.exp(lg - jnp.max(lg, axis=0, keepdims=True))
    probs = ex / jnp.sum(ex, axis=0, keepdims=True)
    lb = jnp.sum(probs[0:e + 1], axis=0, keepdims=True) - probs[0:1]

    rg = _iota2((HGRN_GROUP, HGRN_GROUP), 0)
    cg = _iota2((HGRN_GROUP, HGRN_GROUP), 1)
    same_chunk = (rg // CHUNK) == (cg // CHUNK)
    m01 = lambda cond: jnp.where(same_chunk & cond, 1.0, 0.0).astype(BF16)
    tri_incl = m01(cg <= rg)
    r64 = _iota2((CHUNK, CHUNK), 0)
    c64 = _iota2((CHUNK, CHUNK), 1)
    tri64 = jnp.where(c64 <= r64, 1.0, 0.0).astype(BF16)
    lane2 = _iota2((CHUNK, LANES), 1)
    causal2 = (lane2 % CHUNK) <= _iota2((CHUNK, LANES), 0)
    rb = _iota2((tb, tb), 0)
    cb_ = _iota2((tb, tb), 1)
    tri_rows = jnp.where(((rb // CHUNK) == (cb_ // CHUNK)) & (rb <= cb_), 1.0, 0.0).astype(BF16)
    n_pair = SSD_HEADS // 2
    er = _iota2((LANES, sd), 0)
    ehead = jnp.where(er < n_pair, 2 * er, 2 * (er - n_pair) + 1)
    expand = jnp.where((_iota2((LANES, sd), 1) // SSD_HEADDIM == ehead) & (er < SSD_HEADS), 1.0, 0.0).astype(BF16)
    ones_k = jnp.ones((HGRN_DK, HGRN_DK), BF16)
    pair_diag = (_iota2((LANES, LANES), 0) // SSD_HEADDIM) == (_iota2((LANES, LANES), 1) // SSD_HEADDIM)

    for g in range(n_groups):
        rows = slice(g * HGRN_GROUP, (g + 1) * HGRN_GROUP)
        orow = rows
        fr = pr_ref[rows, c_f:c_f + hk]
        lf = jnp.log(lb + (1.0 - lb) * _sigmoid(fr)) * LOG2E
        kk = (1.0 - lb) * _sigmoid(-fr)
        bcs = _dot01(tri_incl, lf)
        nb = CHUNK // SUB
        n_blk = HGRN_GROUP // SUB
        ends = [bcs[(i + 1) * SUB - 1:(i + 1) * SUB, :] for i in range(n_blk)]
        zero_row = jnp.zeros((1, hk), F32)

        def at_block_end(back):
            rows_ = [ends[i - back] if (i % nb) - back >= 0 else zero_row for i in range(n_blk)]
            return jnp.concatenate([jnp.broadcast_to(r_, (SUB, hk)) for r_ in rows_], axis=0)

        btot = jnp.concatenate([jnp.broadcast_to(ends[(i // nb) * nb + nb - 1], (SUB, hk)) for i in range(n_blk)], axis=0)
        qq = pr_ref[rows, c_q:c_q + hk] * (HGRN_DK ** -0.5)
        vv = pr_ref[rows, c_i:c_i + hk]
        qd_ref[orow, :] = (qq * jnp.exp2(bcs)).astype(BF16)
        kd_ref[orow, :] = (kk * jnp.exp2(btot - bcs)).astype(BF16)
        bd_ref[orow, :] = jnp.exp2(btot)
        kx_ref[orow, :] = (kk * jnp.exp2(at_block_end(0) - bcs)).astype(BF16)
        for m in range(1, nb):
            qx_ref[m - 1, orow, :] = (qq * jnp.exp2(bcs - at_block_end(m))).astype(BF16)
        for h in range(HGRN_HEADS):
            hs = slice(h * HGRN_DK, (h + 1) * HGRN_DK)
            for blk in range(SUBLANES):
                src = slice(blk * SUB, (blk + 1) * SUB)
                lo = (h * n_groups + g) * GROUP_PITCH + blk * BLOCK_PITCH
                qs_ref[lo:lo + SUB, :] = qq[src, hs]
                ks_ref[lo:lo + SUB, :] = kk[src, hs]
                bs_ref[lo:lo + SUB, :] = bcs[src, hs]
                vs_ref[lo:lo + SUB, :] = vv[src, hs]

    pr_ref[:, c_g:] = _dot(hb, win_ref[:, c_g:])

    def pair_step(idx, carry):
        sb = pl.multiple_of(idx * GROUP_PITCH, SUBLANES)
        ob = pl.multiple_of(idx * HGRN_GROUP, HGRN_GROUP)
        ld = lambda ref, r: ref[pl.ds(sb + r, SUBLANES, stride=BLOCK_PITCH), :]
        qv = [ld(qs_ref, r) for r in range(SUB)]
        kv = [ld(ks_ref, r) for r in range(SUB)]
        bv = [ld(bs_ref, r) for r in range(SUB)]
        vv_ = [ld(vs_ref, r) for r in range(SUB)]
        prods = []
        for r in range(SUB):
            for s_ in range(r + 1):
                if s_ == r:
                    prods.append(qv[r] * kv[r])
                else:
                    prods.append(qv[r] * kv[s_] * jnp.exp2(bv[r] - bv[s_]))
        sc = _dot(jnp.concatenate(prods, axis=0).astype(BF16), ones_k)
        n = 0
        for r in range(SUB):
            acc = None
            for s_ in range(r + 1):
                term = sc[n * SUBLANES:(n + 1) * SUBLANES, :] * vv_[s_]
                acc = term if acc is None else acc + term
                n += 1
            od_ref[pl.ds(ob + r, SUBLANES, stride=SUB), :] = acc
        return carry

    lax.fori_loop(0, HGRN_HEADS * n_groups, pair_step, 0, unroll=2)

    for s in range(n_slab):
        su_ref[s * p_u + hist:s * p_u + hist + tb, :] = pr_ref[:, c_x + s * LANES:c_x + (s + 1) * LANES]
    _slab_conv(su_ref, sc_ref, [cw_ref[j] for j in range(taps)], cb_ref[...], hist=hist, n_t=tb, t_step=t_step)
    for s in range(n_slab):
        su_ref[s * p_u:s * p_u + hist, :] = su_ref[s * p_u + tb:s * p_u + tb + hist, :]
    xbc_ref[...] = _silu(jnp.concatenate([sc_ref[s * p_c:s * p_c + tb, :] for s in range(n_slab)], axis=-1))
    dt = jnp.logaddexp(pr_ref[:, c_dt:c_dt + LANES] + dtb_ref[...], 0.0)
    da = dt * (-jnp.exp(alog_ref[...]))
    d0, d1, d2 = _split3(da)
    for c in range(n_chunks):
        cr = slice(c * CHUNK, (c + 1) * CHUNK)
        cs_ref[cr, :] = _dot(tri64, d0[cr]) + _dot(tri64, d1[cr]) + _dot(tri64, d2[cr])
    cs = cs_ref[...]
    cs_end = jnp.concatenate(
        [jnp.broadcast_to(cs[(c + 1) * CHUNK - 1:(c + 1) * CHUNK, :], (CHUNK, LANES)) for c in range(n_chunks)], axis=0)
    cs_t = _dot_tn(d0, tri_rows) + _dot_tn(d1, tri_rows) + _dot_tn(d2, tri_rows)
    for c in range(n_chunks):
        cl = slice(c * CHUNK, (c + 1) * CHUNK)
        csr_ref[c] = jnp.concatenate([cs_t[0:SUBLANES, cl], cs_t[n_pair:n_pair + SUBLANES, cl]], axis=1)
    e0, e1, e2 = _split3(jnp.exp(cs))
    spread = _dot(jnp.concatenate([dt.astype(BF16), jnp.exp(cs_end - cs).astype(BF16), e0, e1, e2], axis=0), expand)
    xc = xbc_ref[:, 0:sd] * spread[0:tb]
    xc_ref[...] = xc.astype(BF16)
    xe_ref[...] = (xc * spread[tb:2 * tb]).astype(BF16)
    ecs_ref[...] = spread[2 * tb:3 * tb] + spread[3 * tb:4 * tb] + spread[4 * tb:5 * tb]

    def local_step(it, carry):
        nb = CHUNK // SUB
        lane64 = _iota2((CHUNK, CHUNK), 1)
        chunks = [it * LOCAL_UNROLL + u for u in range(LOCAL_UNROLL)]
        r0s = [pl.multiple_of(c * CHUNK, CHUNK) for c in chunks]
        stage1 = []
        for c, r0 in zip(chunks, r0s):
            rows = pl.ds(r0, CHUNK)
            res = []
            for h in range(HGRN_HEADS):
                hs = slice(h * HGRN_DK, (h + 1) * HGRN_DK)
                lhs = jnp.concatenate(
                    [qx_ref[i - j - 1, pl.ds(r0 + i * SUB, SUB), hs] for j in range(nb - 1) for i in range(j + 1, nb)],
                    axis=0)
                res.append(_dot_nt(lhs, kx_ref[rows, hs]))
            cbm2 = []
            for g in range(SSD_GROUPS):
                bm = xbc_ref[rows, sd + g * SSD_STATE:sd + (g + 1) * SSD_STATE].astype(BF16)
                cm = xbc_ref[rows, sd + gn + g * SSD_STATE:sd + gn + (g + 1) * SSD_STATE].astype(BF16)
                cbm2.append(_dot_nt(cm, jnp.concatenate([bm, bm], axis=0)))
            stage1.append((res, cbm2))
        for (c, r0), (res, cbm2) in zip(zip(chunks, r0s), stage1):
            rows = pl.ds(r0, CHUNK)
            for h in range(HGRN_HEADS):
                vi = pr_ref[rows, c_i + h * HGRN_DK:c_i + (h + 1) * HGRN_DK].astype(BF16)
                sc = jnp.zeros((CHUNK, CHUNK), F32)
                off = 0
                for j in range(nb - 1):
                    n_r = (nb - 1 - j) * SUB
                    col = jnp.concatenate([jnp.zeros((CHUNK - n_r, CHUNK), F32), res[h][off:off + n_r, :]], axis=0)
                    sc = jnp.where((lane64 // SUB) == j, col, sc)
                    off += n_r
                oc_ref[pl.ds(h * tb + r0, CHUNK), :] = _dot(sc.astype(BF16), vi)
            cs_c = cs_ref[rows, :]
            cs_rows = csr_ref[c]
            for p in range(n_pair):
                cs_pair = jnp.where(lane2 < CHUNK, cs_c[:, p:p + 1], cs_c[:, n_pair + p:n_pair + p + 1])
                dec = jnp.where(causal2, jnp.exp(cs_pair - cs_rows[p:p + 1, :]), 0.0)
                xp = xc_ref[rows, p * LANES:(p + 1) * LANES]
                rhs = jnp.where(pair_diag, jnp.concatenate([xp, xp], axis=0), jnp.zeros((LANES, LANES), BF16))
                g = p // (n_pair // SSD_GROUPS)
                yd_ref[rows, p * LANES:(p + 1) * LANES] = _dot((cbm2[g] * dec).astype(BF16), rhs)
        return carry

    lax.fori_loop(0, n_chunks // LOCAL_UNROLL, local_step, 0)

    def state_step(c, carry):
        r0 = pl.multiple_of(c * CHUNK, CHUNK)
        rows = pl.ds(r0, CHUNK)
        gate = _silu(pr_ref[rows, c_g:c_g + hk])
        for h in range(HGRN_HEADS):
            hs = slice(h * HGRN_DK, (h + 1) * HGRN_DK)
            vi = pr_ref[rows, c_i + h * HGRN_DK:c_i + (h + 1) * HGRN_DK].astype(BF16)
            st = sth_ref[h]
            orow = pl.ds(h * tb + r0, CHUNK)
            o_h = od_ref[orow, :] + oc_ref[orow, :] + _dot_nt(qd_ref[rows, hs], st.astype(BF16))
            sth_ref[h] = st * bd_ref[pl.ds(r0, 1), hs] + _dot_tn(vi, kd_ref[rows, hs])
            o_h = _rms(o_h) * onw_ref[:, hs] * gate[:, hs]
            y_ref[rows, hs] = o_h.astype(BF16)

        xbc = xbc_ref[rows, :]
        ecs = ecs_ref[rows, :]
        xcd = xe_ref[rows, :]
        ys = []
        for g in range(SSD_GROUPS):
            bm = xbc[:, sd + g * SSD_STATE:sd + (g + 1) * SSD_STATE].astype(BF16)
            cm = xbc[:, sd + gn + g * SSD_STATE:sd + gn + (g + 1) * SSD_STATE].astype(BF16)
            gl = slice(g * hp, (g + 1) * hp)
            st = sts_ref[g]
            ys.append(_dot(cm, st.astype(BF16)) * ecs[:, gl])
            sts_ref[g] = st * ecs[CHUNK - 1:CHUNK, gl] + _dot_tn(bm, xcd[:, gl])
        y = jnp.concatenate(ys, axis=-1) + yd_ref[rows, :] + xbc[:, 0:sd] * dsk_ref[...]
        y = y * _silu(pr_ref[rows, c_z:c_z + sd])
        y = jnp.concatenate([_rms(y[:, g * hp:(g + 1) * hp]) for g in range(SSD_GROUPS)], axis=-1) * snw_ref[...]
        y_ref[rows, hk:hk + sd] = y.astype(BF16)
        return carry

    lax.fori_loop(0, n_chunks, state_step, 0, unroll=4)

    o_ref[0] = x + _dot(y_ref[...], wout_ref[...])


def _mixer_weights(w_in, dt_bias, a_log, d_skip):
    ab_in = w_in.shape[2]
    pw = -(-ab_in // LANES) * LANES
    n_dt = SSD_HEADS
    order = list(range(0, n_dt, 2)) + list(range(1, n_dt, 2))
    w_dt = w_in[:, :, ab_in - n_dt:][:, :, order]
    w_in_p = jnp.pad(jnp.concatenate([w_in[:, :, :ab_in - n_dt], w_dt], axis=2),
                     ((0, 0), (0, 0), (0, pw - ab_in))).astype(BF16)
    pad8 = lambda a: _rows3(jnp.pad(a[:, order], ((0, 0), (0, LANES - n_dt))))
    return w_in_p, pad8(dt_bias), pad8(a_log), _rows3(jnp.repeat(d_skip, SSD_HEADDIM, axis=1))


def _mixer(x, norm_w, w_in_p, lb_logits, out_norm_w, conv_w, conv_b, dt_bias_p, a_log_p, d_skip_x, ssd_norm_w,
           w_out, layer, e, *, tb=512, t_step=16):
    b, t, d = x.shape
    n_even, hk = lb_logits.shape
    sd = SSD_HEADS * SSD_HEADDIM
    pw = w_in_p.shape[2]
    taps, cc = conv_w.shape[1:]
    n_slab = cc // LANES
    assert tb % HGRN_GROUP == 0 and n_slab == SUBLANES and taps - 1 <= PAD_ROWS
    p_u, p_c = _slab_pitches(PAD_ROWS, tb)
    slab_rows = HGRN_HEADS * (tb // HGRN_GROUP) * GROUP_PITCH
    slab = pltpu.VMEM((slab_rows, LANES), F32)
    return pl.pallas_call(
        functools.partial(_mix_body, e=e, t_step=t_step),
        grid=(b, t // tb),
        in_specs=[
            pl.BlockSpec((1, tb, d), lambda i, j: (i, j, 0)),
            _layer((1, d), layer),
            _layer((d, pw), e),
            _resident((n_even, hk)),
            _layer((1, hk), e),
            _layer((taps, n_slab, LANES), e),
            _layer((n_slab, LANES), e),
            _layer((1, LANES), e),
            _layer((1, LANES), e),
            _layer((1, sd), e),
            _layer((1, sd), e),
            _layer((hk + sd, d), e),
        ],
        out_specs=pl.BlockSpec((1, tb, d), lambda i, j: (i, j, 0)),
        out_shape=jax.ShapeDtypeStruct((b, t, d), F32),
        scratch_shapes=[
            pltpu.VMEM((tb, pw), F32),
            slab, slab, slab, slab,
            pltpu.VMEM((HGRN_HEADS * tb, LANES), F32),
            pltpu.VMEM((tb, hk), BF16),
            pltpu.VMEM((tb, hk), BF16),
            pltpu.VMEM((tb, hk), F32),
            pltpu.VMEM((CHUNK // SUB - 1, tb, hk), BF16),
            pltpu.VMEM((tb, hk), BF16),
            pltpu.VMEM((n_slab * p_u, LANES), F32),
            pltpu.VMEM((n_slab * p_c, LANES), F32),
            pltpu.VMEM((tb, cc), F32),
            pltpu.VMEM((tb, sd), BF16),
            pltpu.VMEM((tb, sd), BF16),
            pltpu.VMEM((tb, sd), F32),
            pltpu.VMEM((tb, LANES), F32),
            pltpu.VMEM((tb // CHUNK, SUBLANES, LANES), F32),
            pltpu.VMEM((tb, sd), F32),
            pltpu.VMEM((HGRN_HEADS * tb, LANES), F32),
            pltpu.VMEM((HGRN_HEADS, HGRN_DK, HGRN_DK), F32),
            pltpu.VMEM((SSD_GROUPS, SSD_STATE, sd // SSD_GROUPS), F32),
            pltpu.VMEM((tb, hk + sd), BF16),
        ],
        compiler_params=_params(2),
        name="hgrn_ssd_mixer",
    )(x, norm_w, w_in_p, lb_logits, _rows3(out_norm_w), conv_w.reshape(-1, taps, n_slab, LANES),
      conv_b.reshape(-1, n_slab, LANES), dt_bias_p, a_log_p, d_skip_x, _rows3(ssd_norm_w), w_out)


def kernel(x, mem, mem_norm_w, norm_mix_w, ab_w_in, hgrn_lb_logits, hgrn_out_norm_w, ssd_conv_w, ssd_conv_b,
           ssd_dt_bias, ssd_a_log, ssd_d, ssd_norm_w, ab_w_out, cv_w_pw1, cv_b_pw1, cv_w_dw, cv_b_dw, cv_ln_w,
           cv_ln_b, cv_w_pw2, cv_b_pw2, norm_xattn_w, xattn_wq, xattn_wk, xattn_wv, xattn_wo, norm_mlp_w,
           mlp_w1, mlp_w2, final_norm_w):
    b, t, d = x.shape
    depth = norm_mix_w.shape[0]
    bf = lambda a: a.astype(BF16)
    w_in_p, dt_bias_p, a_log_p, d_skip_x = _mixer_weights(ab_w_in, ssd_dt_bias, ssd_a_log, ssd_d)
    w_out, w_pw1, w_pw2, wq, wo, w1, w2 = map(bf, (ab_w_out, cv_w_pw1, cv_w_pw2, xattn_wq, xattn_wo, mlp_w1, mlp_w2))
    n_mix, n_att, n_mlp = _rows3(norm_mix_w), _rows3(norm_xattn_w), _rows3(norm_mlp_w)
    k_all, v_all = _memory_kv(mem, mem_norm_w, xattn_wk, xattn_wv)
    for layer in range(depth):
        if layer % 2 == 0:
            e = layer // 2
            x = _mixer(x, n_mix, w_in_p, hgrn_lb_logits, hgrn_out_norm_w, ssd_conv_w, ssd_conv_b, dt_bias_p, a_log_p,
                       d_skip_x, ssd_norm_w, w_out, layer, e)
        else:
            x = _conformer(x, n_mix, w_pw1, cv_b_pw1, cv_w_dw, cv_b_dw, cv_ln_w, cv_ln_b, w_pw2, cv_b_pw2, layer,
                           layer // 2)
        x = _cross_attention(x, n_att, wq, k_all, v_all, wo, layer)
        x = _mlp(x.reshape(b * t, d), n_mlp, w1, w2, final_norm_w, layer,
                 final=(layer == depth - 1)).reshape(b, t, d)
    return x
```

```python
import functools

import jax
import jax.numpy as jnp
from jax import lax
from jax.experimental import pallas as pl
from jax.experimental.pallas import tpu as pltpu

F32 = jnp.float32
BF16 = jnp.bfloat16

EPS = 1e-6
CHUNK = 64
SUB = 16
LANES = 128
SUBLANES = 8
PAD_ROWS = 8

HGRN_HEADS = 4
HGRN_DK = 128
SSD_HEADS = 8
SSD_HEADDIM = 64
SSD_GROUPS = 2
SSD_STATE = 128
XATTN_HEADS = 4

VMEM_LIMIT_BYTES = 56 * 1024 * 1024


def _rms(x):
    return x * lax.rsqrt(jnp.mean(x * x, axis=-1, keepdims=True) + EPS)


def _sigmoid(x):
    return 1.0 / (1.0 + jnp.exp(-x))


def _silu(x):
    return x * _sigmoid(x)


def _dot(a, b):
    return jnp.dot(a, b, preferred_element_type=F32)


def _dot_nt(a, b):
    return lax.dot_general(a, b, (((1,), (1,)), ((), ())), preferred_element_type=F32)


def _dot_tn(a, b):
    return lax.dot_general(a, b, (((0,), (0,)), ((), ())), preferred_element_type=F32)


def _split3(a):
    a0 = a.astype(BF16)
    r1 = a - a0.astype(F32)
    a1 = r1.astype(BF16)
    a2 = (r1 - a1.astype(F32)).astype(BF16)
    return a0, a1, a2


def _dot01(m01, a):
    a0, a1, a2 = _split3(a)
    return _dot(m01, a0) + _dot(m01, a1) + _dot(m01, a2)


def _resident(shape):
    nd = len(shape)
    return pl.BlockSpec(shape, lambda *_: (0,) * nd, pipeline_mode=pl.Buffered(1))


def _layer(shape, layer):
    nd = len(shape)
    return pl.BlockSpec((None,) + tuple(shape), lambda *_: (layer,) + (0,) * nd, pipeline_mode=pl.Buffered(1))


def _rows3(a):
    return a.reshape(a.shape[0], 1, a.shape[-1])


def _params(n_axes):
    return pltpu.CompilerParams(dimension_semantics=("arbitrary",) * n_axes,
                                vmem_limit_bytes=VMEM_LIMIT_BYTES)


def _kv_body(mem_ref, mw_ref, wk_ref, wv_ref, k_ref, v_ref):
    mn = (_rms(mem_ref[...]) * mw_ref[...]).astype(BF16)
    k_ref[0] = _dot(mn, wk_ref[0].astype(BF16)).astype(BF16)
    v_ref[0] = _dot(mn, wv_ref[0].astype(BF16)).astype(BF16)


def _memory_kv(mem, mem_norm_w, wk, wv):
    depth, d, _ = wk.shape
    b, m, _ = mem.shape
    out = jax.ShapeDtypeStruct((depth, b * m, d), BF16)
    k_all, v_all = pl.pallas_call(
        _kv_body,
        grid=(depth,),
        in_specs=[
            _resident((b * m, d)),
            _resident((1, d)),
            pl.BlockSpec((1, d, d), lambda l: (l, 0, 0)),
            pl.BlockSpec((1, d, d), lambda l: (l, 0, 0)),
        ],
        out_specs=[
            pl.BlockSpec((1, b * m, d), lambda l: (l, 0, 0)),
            pl.BlockSpec((1, b * m, d), lambda l: (l, 0, 0)),
        ],
        out_shape=[out, out],
        compiler_params=_params(1),
        name="memory_kv",
    )(mem.reshape(b * m, d), mem_norm_w.reshape(1, d), wk, wv)
    return k_all.reshape(depth, b, m, d), v_all.reshape(depth, b, m, d)


def _mlp_body(x_ref, nw_ref, w1_ref, w2_ref, fw_ref, o_ref, *, ff_chunk, final):
    x = x_ref[...]
    hb = (_rms(x) * nw_ref[...]).astype(BF16)
    acc = x
    for c in range(w1_ref.shape[1] // ff_chunk):
        a = _dot(hb, w1_ref[:, c * ff_chunk:(c + 1) * ff_chunk])
        a = jnp.square(jnp.maximum(a, 0.0)).astype(BF16)
        acc = acc + _dot(a, w2_ref[c * ff_chunk:(c + 1) * ff_chunk, :])
    if final:
        acc = _rms(acc) * fw_ref[...]
    o_ref[...] = acc


def _mlp(x2, norm_w, w1, w2, final_w, layer, *, final, tb=1024, ff_chunk=1024):
    n, d = x2.shape
    dff = w1.shape[2]
    return pl.pallas_call(
        functools.partial(_mlp_body, ff_chunk=ff_chunk, final=final),
        grid=(n // tb,),
        in_specs=[
            pl.BlockSpec((tb, d), lambda i: (i, 0)),
            _layer((1, d), layer),
            _layer((d, dff), layer),
            _layer((dff, d), layer),
            _resident((1, d)),
        ],
        out_specs=pl.BlockSpec((tb, d), lambda i: (i, 0)),
        out_shape=jax.ShapeDtypeStruct((n, d), F32),
        compiler_params=_params(1),
        name="sq_relu_mlp",
    )(x2, norm_w, w1, w2, final_w.reshape(1, d))


def _attn_body(x_ref, nw_ref, wq_ref, k_ref, v_ref, wo_ref, o_ref, *, heads):
    x = x_ref[0]
    d = x.shape[-1]
    hd = d // heads
    hb = (_rms(x) * nw_ref[...]).astype(BF16)
    q = (_dot(hb, wq_ref[...]) * (hd ** -0.5)).astype(BF16)
    outs = []
    for h in range(heads):
        sl = slice(h * hd, (h + 1) * hd)
        s = _dot_nt(q[:, sl], k_ref[0, 0, :, sl])
        p = jnp.exp(s - jnp.max(s, axis=-1, keepdims=True))
        den = jnp.sum(p, axis=-1, keepdims=True)
        oh = _dot(p.astype(BF16), v_ref[0, 0, :, sl]) * (1.0 / den)
        outs.append(oh.astype(BF16))
    o = jnp.concatenate(outs, axis=-1)
    o_ref[0] = x + _dot(o, wo_ref[...])


def _cross_attention(x, norm_w, wq, k_all, v_all, wo, layer, *, tb=1024):
    b, t, d = x.shape
    m = k_all.shape[2]
    return pl.pallas_call(
        functools.partial(_attn_body, heads=XATTN_HEADS),
        grid=(b, t // tb),
        in_specs=[
            pl.BlockSpec((1, tb, d), lambda i, j: (i, j, 0)),
            _layer((1, d), layer),
            _layer((d, d), layer),
            pl.BlockSpec((1, 1, m, d), lambda i, j: (layer, i, 0, 0)),
            pl.BlockSpec((1, 1, m, d), lambda i, j: (layer, i, 0, 0)),
            _layer((d, d), layer),
        ],
        out_specs=pl.BlockSpec((1, tb, d), lambda i, j: (i, j, 0)),
        out_shape=jax.ShapeDtypeStruct((b, t, d), F32),
        compiler_params=_params(2),
        name="memory_xattn",
    )(x, norm_w, wq, k_all, v_all, wo)


def _slab_conv(u_ref, c_ref, w_taps, bias, *, hist, n_t, t_step, t_lo=None):
    taps = len(w_taps)
    n_slab = bias.shape[0]
    p_u = u_ref.shape[0] // n_slab
    p_c = c_ref.shape[0] // n_slab

    def step(i, carry):
        t0 = i * t_step if isinstance(i, int) else pl.multiple_of(i * t_step, t_step)
        accs = [bias] * t_step
        for k in range(t_step + taps - 1):
            uk = u_ref[pl.ds(t0 + hist - (taps - 1) + k, n_slab, stride=p_u), :]
            for tt in range(t_step):
                if 0 <= k - tt < taps:
                    accs[tt] = accs[tt] + w_taps[k - tt] * uk
        for tt in range(t_step):
            c_ref[pl.ds(t0 + tt, n_slab, stride=p_c), :] = accs[tt]
        return carry

    if t_lo is None:
        lax.fori_loop(0, n_t // t_step, step, 0)
    else:
        for i in range(t_lo // t_step, (t_lo + n_t) // t_step):
            step(i, 0)


def _slab_pitches(hist, n_t):
    return hist + n_t + 4, n_t + SUBLANES


def _conf_body(x_ref, nw_ref, w1_ref, b1_ref, wdw_ref, bdw_ref, lnw_ref, lnb_ref, w2_ref, b2_ref,
               o_ref, u_ref, c_ref, *, taps, hist, t_step):
    tb, d = x_ref.shape[1], x_ref.shape[2]
    n_slab = d // LANES
    p_u = u_ref.shape[0] // n_slab
    p_c = c_ref.shape[0] // n_slab

    @pl.when(pl.program_id(1) == 0)
    def _():
        for s in range(n_slab):
            u_ref[s * p_u:s * p_u + hist, :] = jnp.zeros((hist, LANES), F32)

    half = tb // 2
    w_taps = [wdw_ref[j] for j in range(taps)]

    def glu_to_slabs(ag, lo):
        u = ag[:, :d] * _sigmoid(ag[:, d:])
        for s in range(n_slab):
            u_ref[s * p_u + hist + lo:s * p_u + hist + lo + half, :] = u[:, s * LANES:(s + 1) * LANES]

    def finish(xh, lo):
        c = jnp.concatenate([c_ref[s * p_c + lo:s * p_c + lo + half, :] for s in range(n_slab)], axis=-1)
        mu = jnp.mean(c, axis=-1, keepdims=True)
        cc = c - mu
        var = jnp.mean(cc * cc, axis=-1, keepdims=True)
        yn = cc * lax.rsqrt(var + EPS) * lnw_ref[...] + lnb_ref[...]
        ys = _silu(yn).astype(BF16)
        o_ref[0, lo:lo + half, :] = xh + _dot(ys, w2_ref[...]) + b2_ref[...]

    x0 = x_ref[0, 0:half, :]
    x1 = x_ref[0, half:tb, :]
    ag0 = _dot((_rms(x0) * nw_ref[...]).astype(BF16), w1_ref[...]) + b1_ref[...]
    glu_to_slabs(ag0, 0)
    ag1 = _dot((_rms(x1) * nw_ref[...]).astype(BF16), w1_ref[...]) + b1_ref[...]
    _slab_conv(u_ref, c_ref, w_taps, bdw_ref[...], hist=hist, n_t=half, t_step=t_step, t_lo=0)
    glu_to_slabs(ag1, half)
    finish(x0, 0)
    _slab_conv(u_ref, c_ref, w_taps, bdw_ref[...], hist=hist, n_t=half, t_step=t_step, t_lo=half)
    for s in range(n_slab):
        u_ref[s * p_u:s * p_u + hist, :] = u_ref[s * p_u + tb:s * p_u + tb + hist, :]
    finish(x1, half)


def _conformer(x, norm_w, w_pw1, b_pw1, w_dw, b_dw, ln_w, ln_b, w_pw2, b_pw2, layer, o, *, tb=1024, t_step=16):
    b, t, d = x.shape
    taps = w_dw.shape[1]
    n_slab = d // LANES
    assert n_slab == SUBLANES
    hist = -(-(taps - 1) // SUBLANES) * SUBLANES
    p_u, p_c = _slab_pitches(hist, tb)
    return pl.pallas_call(
        functools.partial(_conf_body, taps=taps, hist=hist, t_step=t_step),
        grid=(b, t // tb),
        in_specs=[
            pl.BlockSpec((1, tb, d), lambda i, j: (i, j, 0)),
            _layer((1, d), layer),
            _layer((d, 2 * d), o),
            _layer((1, 2 * d), o),
            _layer((taps, n_slab, LANES), o),
            _layer((n_slab, LANES), o),
            _layer((1, d), o),
            _layer((1, d), o),
            _layer((d, d), o),
            _layer((1, d), o),
        ],
        out_specs=pl.BlockSpec((1, tb, d), lambda i, j: (i, j, 0)),
        out_shape=jax.ShapeDtypeStruct((b, t, d), F32),
        scratch_shapes=[pltpu.VMEM((n_slab * p_u, LANES), F32), pltpu.VMEM((n_slab * p_c, LANES), F32)],
        compiler_params=_params(2),
        name="conformer_conv",
    )(x, norm_w, w_pw1, _rows3(b_pw1), w_dw.reshape(-1, taps, n_slab, LANES), b_dw.reshape(-1, n_slab, LANES),
      _rows3(ln_w), _rows3(ln_b), w_pw2, _rows3(b_pw2))


def _iota2(shape, axis):
    return lax.broadcasted_iota(jnp.int32, shape, axis)


HGRN_GROUP = SUB * SUBLANES
BLOCK_PITCH = SUB + 4
GROUP_PITCH = SUBLANES * BLOCK_PITCH
LOG2E = 1.4426950408889634
LOCAL_UNROLL = 8


def _mix_body(x_ref, nw_ref, win_ref, lbl_ref, onw_ref, cw_ref, cb_ref, dtb_ref, alog_ref, dsk_ref,
              snw_ref, wout_ref, o_ref,
              pr_ref, qs_ref, ks_ref, bs_ref, vs_ref, od_ref, qd_ref, kd_ref, bd_ref, qx_ref, kx_ref,
              su_ref, sc_ref, xbc_ref, xc_ref, xe_ref, ecs_ref, cs_ref, csr_ref, yd_ref, oc_ref, sth_ref, sts_ref, y_ref, *, e, t_step):
    tb, d = x_ref.shape[1], x_ref.shape[2]
    hk = HGRN_HEADS * HGRN_DK
    sd = SSD_HEADS * SSD_HEADDIM
    gn = SSD_GROUPS * SSD_STATE
    c_q, c_f, c_i, c_g = 0, hk, 2 * hk, 3 * hk
    c_z = 4 * hk
    c_x = c_z + sd
    c_dt = c_x + sd + 2 * gn
    hp = SSD_HEADS // SSD_GROUPS * SSD_HEADDIM
    taps = cw_ref.shape[0]
    hist = PAD_ROWS
    n_slab = cb_ref.shape[0]
    p_u = su_ref.shape[0] // n_slab
    p_c = sc_ref.shape[0] // n_slab
    n_groups = tb // HGRN_GROUP
    n_chunks = tb // CHUNK

    @pl.when(pl.program_id(1) == 0)
    def _():
        for s in range(n_slab):
            su_ref[s * p_u:s * p_u + hist, :] = jnp.zeros((hist, LANES), F32)
        sth_ref[...] = jnp.zeros(sth_ref.shape, F32)
        sts_ref[...] = jnp.zeros(sts_ref.shape, F32)

    x = x_ref[0]
    hb = (_rms(x) * nw_ref[...]).astype(BF16)
    pr_ref[:, 0:c_g] = _dot(hb, win_ref[:, 0:c_g])

    lg = lbl_ref[...]
    ex = jnp.exp(lg - jnp.max(lg, axis=0, keepdims=True))
    probs = ex / jnp.sum(ex, axis=0, keepdims=True)
    lb = jnp.sum(probs[0:e + 1], axis=0, keepdims=True) - probs[0:1]

    rg = _iota2((HGRN_GROUP, HGRN_GROUP), 0)
    cg = _iota2((HGRN_GROUP, HGRN_GROUP), 1)
    same_chunk = (rg // CHUNK) == (cg // CHUNK)
    m01 = lambda cond: jnp.where(same_chunk & cond, 1.0, 0.0).astype(BF16)
    tri_incl = m01(cg <= rg)
    r64 = _iota2((CHUNK, CHUNK), 0)
    c64 = _iota2((CHUNK, CHUNK), 1)
    tri64 = jnp.where(c64 <= r64, 1.0, 0.0).astype(BF16)
    lane2 = _iota2((CHUNK, LANES), 1)
    causal2 = (lane2 % CHUNK) <= _iota2((CHUNK, LANES), 0)
    rb = _iota2((tb, tb), 0)
    cb_ = _iota2((tb, tb), 1)
    tri_rows = jnp.where(((rb // CHUNK) == (cb_ // CHUNK)) & (rb <= cb_), 1.0, 0.0).astype(BF16)
    n_pair = SSD_HEADS // 2
    er = _iota2((LANES, sd), 0)
    expand = jnp.where(_iota2((LANES, sd), 1) // SSD_HEADDIM == er, 1.0, 0.0).astype(BF16)
    ones_k = jnp.ones((HGRN_DK, HGRN_DK), BF16)
    pair_diag = (_iota2((LANES, LANES), 0) // SSD_HEADDIM) == (_iota2((LANES, LANES), 1) // SSD_HEADDIM)

    for g in range(n_groups):
        rows = slice(g * HGRN_GROUP, (g + 1) * HGRN_GROUP)
        orow = rows
        fr = pr_ref[rows, c_f:c_f + hk]
        lf = jnp.log(lb + (1.0 - lb) * _sigmoid(fr)) * LOG2E
        kk = (1.0 - lb) * _sigmoid(-fr)
        bcs = _dot01(tri_incl, lf)
        nb = CHUNK // SUB
        n_blk = HGRN_GROUP // SUB
        ends = [bcs[(i + 1) * SUB - 1:(i + 1) * SUB, :] for i in range(n_blk)]
        zero_row = jnp.zeros((1, hk), F32)

        def at_block_end(back):
            rows_ = [ends[i - back] if (i % nb) - back >= 0 else zero_row for i in range(n_blk)]
            return jnp.concatenate([jnp.broadcast_to(r_, (SUB, hk)) for r_ in rows_], axis=0)

        btot = jnp.concatenate([jnp.broadcast_to(ends[(i // nb) * nb + nb - 1], (SUB, hk)) for i in range(n_blk)], axis=0)
        qq = pr_ref[rows, c_q:c_q + hk] * (HGRN_DK ** -0.5)
        vv = pr_ref[rows, c_i:c_i + hk]
        qd_ref[orow, :] = (qq * jnp.exp2(bcs)).astype(BF16)
        kd_ref[orow, :] = (kk * jnp.exp2(btot - bcs)).astype(BF16)
        bd_ref[orow, :] = jnp.exp2(btot)
        kx_ref[orow, :] = (kk * jnp.exp2(at_block_end(0) - bcs)).astype(BF16)
        for m in range(1, nb):
            qx_ref[m - 1, orow, :] = (qq * jnp.exp2(bcs - at_block_end(m))).astype(BF16)
        for h in range(HGRN_HEADS):
            hs = slice(h * HGRN_DK, (h + 1) * HGRN_DK)
            for blk in range(SUBLANES):
                src = slice(blk * SUB, (blk + 1) * SUB)
                lo = (h * n_groups + g) * GROUP_PITCH + blk * BLOCK_PITCH
                qs_ref[lo:lo + SUB, :] = qq[src, hs]
                ks_ref[lo:lo + SUB, :] = kk[src, hs]
                bs_ref[lo:lo + SUB, :] = bcs[src, hs]
                vs_ref[lo:lo + SUB, :] = vv[src, hs]

    pr_ref[:, c_g:] = _dot(hb, win_ref[:, c_g:])

    def pair_step(idx, carry):
        sb = pl.multiple_of(idx * GROUP_PITCH, SUBLANES)
        ob = pl.multiple_of(idx * HGRN_GROUP, HGRN_GROUP)
        ld = lambda ref, r: ref[pl.ds(sb + r, SUBLANES, stride=BLOCK_PITCH), :]
        qv = [ld(qs_ref, r) for r in range(SUB)]
        kv = [ld(ks_ref, r) for r in range(SUB)]
        bv = [ld(bs_ref, r) for r in range(SUB)]
        vv_ = [ld(vs_ref, r) for r in range(SUB)]
        prods = []
        for r in range(SUB):
            for s_ in range(r + 1):
                if s_ == r:
                    prods.append(qv[r] * kv[r])
                else:
                    prods.append(qv[r] * kv[s_] * jnp.exp2(bv[r] - bv[s_]))
        sc = _dot(jnp.concatenate(prods, axis=0).astype(BF16), ones_k)
        n = 0
        for r in range(SUB):
            acc = None
            for s_ in range(r + 1):
                term = sc[n * SUBLANES:(n + 1) * SUBLANES, :] * vv_[s_]
                acc = term if acc is None else acc + term
                n += 1
            od_ref[pl.ds(ob + r, SUBLANES, stride=SUB), :] = acc
        return carry

    lax.fori_loop(0, HGRN_HEADS * n_groups, pair_step, 0, unroll=2)

    for s in range(n_slab):
        su_ref[s * p_u + hist:s * p_u + hist + tb, :] = pr_ref[:, c_x + s * LANES:c_x + (s + 1) * LANES]
    _slab_conv(su_ref, sc_ref, [cw_ref[j] for j in range(taps)], cb_ref[...], hist=hist, n_t=tb, t_step=t_step)
    for s in range(n_slab):
        su_ref[s * p_u:s * p_u + hist, :] = su_ref[s * p_u + tb:s * p_u + tb + hist, :]
    xbc_ref[...] = _silu(jnp.concatenate([sc_ref[s * p_c:s * p_c + tb, :] for s in range(n_slab)], axis=-1))
    dt = jnp.logaddexp(pr_ref[:, c_dt:c_dt + LANES] + dtb_ref[...], 0.0)
    da = dt * (-jnp.exp(alog_ref[...]))
    d0, d1, d2 = _split3(da)
    for c in range(n_chunks):
        cr = slice(c * CHUNK, (c + 1) * CHUNK)
        cs_ref[cr, :] = _dot(tri64, d0[cr]) + _dot(tri64, d1[cr]) + _dot(tri64, d2[cr])
    cs = cs_ref[...]
    cs_end = jnp.concatenate(
        [jnp.broadcast_to(cs[(c + 1) * CHUNK - 1:(c + 1) * CHUNK, :], (CHUNK, LANES)) for c in range(n_chunks)], axis=0)
    cs_t = _dot_tn(d0, tri_rows) + _dot_tn(d1, tri_rows) + _dot_tn(d2, tri_rows)
    fill = [jnp.zeros((SUBLANES - n_pair, tb), F32)]
    cs_even = jnp.concatenate([cs_t[2 * p:2 * p + 1, :] for p in range(n_pair)] + fill, axis=0)
    cs_odd = jnp.concatenate([cs_t[2 * p + 1:2 * p + 2, :] for p in range(n_pair)] + fill, axis=0)
    for c in range(n_chunks):
        cl = slice(c * CHUNK, (c + 1) * CHUNK)
        csr_ref[c] = jnp.concatenate([cs_even[:, cl], cs_odd[:, cl]], axis=1)
    e0, e1, e2 = _split3(jnp.exp(cs))
    spread = _dot(jnp.concatenate([dt.astype(BF16), jnp.exp(cs_end - cs).astype(BF16), e0, e1, e2], axis=0), expand)
    xc = xbc_ref[:, 0:sd] * spread[0:tb]
    xc_ref[...] = xc.astype(BF16)
    xe_ref[...] = (xc * spread[tb:2 * tb]).astype(BF16)
    ecs_ref[...] = spread[2 * tb:3 * tb] + spread[3 * tb:4 * tb] + spread[4 * tb:5 * tb]

    def local_step(it, carry):
        nb = CHUNK // SUB
        lane64 = _iota2((CHUNK, CHUNK), 1)
        chunks = [it * LOCAL_UNROLL + u for u in range(LOCAL_UNROLL)]
        r0s = [pl.multiple_of(c * CHUNK, CHUNK) for c in chunks]
        stage1 = []
        for c, r0 in zip(chunks, r0s):
            rows = pl.ds(r0, CHUNK)
            res = []
            for h in range(HGRN_HEADS):
                hs = slice(h * HGRN_DK, (h + 1) * HGRN_DK)
                lhs = jnp.concatenate(
                    [qx_ref[i - j - 1, pl.ds(r0 + i * SUB, SUB), hs] for j in range(nb - 1) for i in range(j + 1, nb)],
                    axis=0)
                res.append(_dot_nt(lhs, kx_ref[rows, hs]))
            cbm2 = []
            for g in range(SSD_GROUPS):
                bm = xbc_ref[rows, sd + g * SSD_STATE:sd + (g + 1) * SSD_STATE].astype(BF16)
                cm = xbc_ref[rows, sd + gn + g * SSD_STATE:sd + gn + (g + 1) * SSD_STATE].astype(BF16)
                cbm2.append(_dot_nt(cm, jnp.concatenate([bm, bm], axis=0)))
            stage1.append((res, cbm2))
        for (c, r0), (res, cbm2) in zip(zip(chunks, r0s), stage1):
            rows = pl.ds(r0, CHUNK)
            for h in range(HGRN_HEADS):
                vi = pr_ref[rows, c_i + h * HGRN_DK:c_i + (h + 1) * HGRN_DK].astype(BF16)
                sc = jnp.zeros((CHUNK, CHUNK), F32)
                off = 0
                for j in range(nb - 1):
                    n_r = (nb - 1 - j) * SUB
                    col = jnp.concatenate([jnp.zeros((CHUNK - n_r, CHUNK), F32), res[h][off:off + n_r, :]], axis=0)
                    sc = jnp.where((lane64 // SUB) == j, col, sc)
                    off += n_r
                oc_ref[pl.ds(h * tb + r0, CHUNK), :] = _dot(sc.astype(BF16), vi)
            cs_c = cs_ref[rows, :]
            cs_rows = csr_ref[c]
            for p in range(n_pair):
                cs_pair = jnp.where(lane2 < CHUNK, cs_c[:, 2 * p:2 * p + 1], cs_c[:, 2 * p + 1:2 * p + 2])
                dec = jnp.where(causal2, jnp.exp(cs_pair - cs_rows[p:p + 1, :]), 0.0)
                xp = xc_ref[rows, p * LANES:(p + 1) * LANES]
                rhs = jnp.where(pair_diag, jnp.concatenate([xp, xp], axis=0), jnp.zeros((LANES, LANES), BF16))
                g = p // (n_pair // SSD_GROUPS)
                yd_ref[rows, p * LANES:(p + 1) * LANES] = _dot((cbm2[g] * dec).astype(BF16), rhs)
        return carry

    lax.fori_loop(0, n_chunks // LOCAL_UNROLL, local_step, 0)

    def state_step(c, carry):
        r0 = pl.multiple_of(c * CHUNK, CHUNK)
        rows = pl.ds(r0, CHUNK)
        gate = _silu(pr_ref[rows, c_g:c_g + hk])
        for h in range(HGRN_HEADS):
            hs = slice(h * HGRN_DK, (h + 1) * HGRN_DK)
            vi = pr_ref[rows, c_i + h * HGRN_DK:c_i + (h + 1) * HGRN_DK].astype(BF16)
            st = sth_ref[h]
            orow = pl.ds(h * tb + r0, CHUNK)
            o_h = od_ref[orow, :] + oc_ref[orow, :] + _dot_nt(qd_ref[rows, hs], st.astype(BF16))
            sth_ref[h] = st * bd_ref[pl.ds(r0, 1), hs] + _dot_tn(vi, kd_ref[rows, hs])
            o_h = _rms(o_h) * onw_ref[:, hs] * gate[:, hs]
            y_ref[rows, hs] = o_h.astype(BF16)

        xbc = xbc_ref[rows, :]
        ecs = ecs_ref[rows, :]
        xcd = xe_ref[rows, :]
        ys = []
        for g in range(SSD_GROUPS):
            bm = xbc[:, sd + g * SSD_STATE:sd + (g + 1) * SSD_STATE].astype(BF16)
            cm = xbc[:, sd + gn + g * SSD_STATE:sd + gn + (g + 1) * SSD_STATE].astype(BF16)
            gl = slice(g * hp, (g + 1) * hp)
            st = sts_ref[g]
            ys.append(_dot(cm, st.astype(BF16)) * ecs[:, gl])
            sts_ref[g] = st * ecs[CHUNK - 1:CHUNK, gl] + _dot_tn(bm, xcd[:, gl])
        y = jnp.concatenate(ys, axis=-1) + yd_ref[rows, :] + xbc[:, 0:sd] * dsk_ref[...]
        y = y * _silu(pr_ref[rows, c_z:c_z + sd])
        y = jnp.concatenate([_rms(y[:, g * hp:(g + 1) * hp]) for g in range(SSD_GROUPS)], axis=-1) * snw_ref[...]
        y_ref[rows, hk:hk + sd] = y.astype(BF16)
        return carry

    lax.fori_loop(0, n_chunks, state_step, 0, unroll=4)

    o_ref[0] = x + _dot(y_ref[...], wout_ref[...])


def _mixer_weights(w_in, dt_bias, a_log, d_skip):
    ab_in = w_in.shape[2]
    pw = -(-ab_in // LANES) * LANES
    w_in_p = jnp.pad(w_in, ((0, 0), (0, 0), (0, pw - ab_in))).astype(BF16)
    pad8 = lambda a: _rows3(jnp.pad(a, ((0, 0), (0, LANES - a.shape[1]))))
    return w_in_p, pad8(dt_bias), pad8(a_log), _rows3(jnp.repeat(d_skip, SSD_HEADDIM, axis=1))


def _mixer(x, norm_w, w_in_p, lb_logits, out_norm_w, conv_w, conv_b, dt_bias_p, a_log_p, d_skip_x, ssd_norm_w,
           w_out, layer, e, *, tb=512, t_step=16):
    b, t, d = x.shape
    n_even, hk = lb_logits.shape
    sd = SSD_HEADS * SSD_HEADDIM
    pw = w_in_p.shape[2]
    taps, cc = conv_w.shape[1:]
    n_slab = cc // LANES
    assert tb % HGRN_GROUP == 0 and n_slab == SUBLANES and taps - 1 <= PAD_ROWS
    p_u, p_c = _slab_pitches(PAD_ROWS, tb)
    slab_rows = HGRN_HEADS * (tb // HGRN_GROUP) * GROUP_PITCH
    slab = pltpu.VMEM((slab_rows, LANES), F32)
    return pl.pallas_call(
        functools.partial(_mix_body, e=e, t_step=t_step),
        grid=(b, t // tb),
        in_specs=[
            pl.BlockSpec((1, tb, d), lambda i, j: (i, j, 0)),
            _layer((1, d), layer),
            _layer((d, pw), e),
            _resident((n_even, hk)),
            _layer((1, hk), e),
            _layer((taps, n_slab, LANES), e),
            _layer((n_slab, LANES), e),
            _layer((1, LANES), e),
            _layer((1, LANES), e),
            _layer((1, sd), e),
            _layer((1, sd), e),
            _layer((hk + sd, d), e),
        ],
        out_specs=pl.BlockSpec((1, tb, d), lambda i, j: (i, j, 0)),
        out_shape=jax.ShapeDtypeStruct((b, t, d), F32),
        scratch_shapes=[
            pltpu.VMEM((tb, pw), F32),
            slab, slab, slab, slab,
            pltpu.VMEM((HGRN_HEADS * tb, LANES), F32),
            pltpu.VMEM((tb, hk), BF16),
            pltpu.VMEM((tb, hk), BF16),
            pltpu.VMEM((tb, hk), F32),
            pltpu.VMEM((CHUNK // SUB - 1, tb, hk), BF16),
            pltpu.VMEM((tb, hk), BF16),
            pltpu.VMEM((n_slab * p_u, LANES), F32),
            pltpu.VMEM((n_slab * p_c, LANES), F32),
            pltpu.VMEM((tb, cc), F32),
            pltpu.VMEM((tb, sd), BF16),
            pltpu.VMEM((tb, sd), BF16),
            pltpu.VMEM((tb, sd), F32),
            pltpu.VMEM((tb, LANES), F32),
            pltpu.VMEM((tb // CHUNK, SUBLANES, LANES), F32),
            pltpu.VMEM((tb, sd), F32),
            pltpu.VMEM((HGRN_HEADS * tb, LANES), F32),
            pltpu.VMEM((HGRN_HEADS, HGRN_DK, HGRN_DK), F32),
            pltpu.VMEM((SSD_GROUPS, SSD_STATE, sd // SSD_GROUPS), F32),
            pltpu.VMEM((tb, hk + sd), BF16),
        ],
        compiler_params=_params(2),
        name="hgrn_ssd_mixer",
    )(x, norm_w, w_in_p, lb_logits, _rows3(out_norm_w), conv_w.reshape(-1, taps, n_slab, LANES),
      conv_b.reshape(-1, n_slab, LANES), dt_bias_p, a_log_p, d_skip_x, _rows3(ssd_norm_w), w_out)


def kernel(x, mem, mem_norm_w, norm_mix_w, ab_w_in, hgrn_lb_logits, hgrn_out_norm_w, ssd_conv_w, ssd_conv_b,
           ssd_dt_bias, ssd_a_log, ssd_d, ssd_norm_w, ab_w_out, cv_w_pw1, cv_b_pw1, cv_w_dw, cv_b_dw, cv_ln_w,
           cv_ln_b, cv_w_pw2, cv_b_pw2, norm_xattn_w, xattn_wq, xattn_wk, xattn_wv, xattn_wo, norm_mlp_w,
           mlp_w1, mlp_w2, final_norm_w):
    b, t, d = x.shape
    depth = norm_mix_w.shape[0]
    bf = lambda a: a.astype(BF16)
    w_in_p, dt_bias_p, a_log_p, d_skip_x = _mixer_weights(ab_w_in, ssd_dt_bias, ssd_a_log, ssd_d)
    w_out, w_pw1, w_pw2, wq, wo, w1, w2 = map(bf, (ab_w_out, cv_w_pw1, cv_w_pw2, xattn_wq, xattn_wo, mlp_w1, mlp_w2))
    n_mix, n_att, n_mlp = _rows3(norm_mix_w), _rows3(norm_xattn_w), _rows3(norm_mlp_w)
    k_all, v_all = _memory_kv(mem, mem_norm_w, xattn_wk, xattn_wv)
    for layer in range(depth):
        if layer % 2 == 0:
            e = layer // 2
            x = _mixer(x, n_mix, w_in_p, hgrn_lb_logits, hgrn_out_norm_w, ssd_conv_w, ssd_conv_b, dt_bias_p, a_log_p,
                       d_skip_x, ssd_norm_w, w_out, layer, e)
        else:
            x = _conformer(x, n_mix, w_pw1, cv_b_pw1, cv_w_dw, cv_b_dw, cv_ln_w, cv_ln_b, w_pw2, cv_b_pw2, layer,
                           layer // 2)
        x = _cross_attention(x, n_att, wq, k_all, v_all, wo, layer)
        x = _mlp(x.reshape(b * t, d), n_mlp, w1, w2, final_norm_w, layer,
                 final=(layer == depth - 1)).reshape(b, t, d)
    return x
```

```python
import functools

import jax
import jax.numpy as jnp
from jax import lax
from jax.experimental import pallas as pl
from jax.experimental.pallas import tpu as pltpu

F32 = jnp.float32
BF16 = jnp.bfloat16

EPS = 1e-6
CHUNK = 64
SUB = 16
LANES = 128
SUBLANES = 8
PAD_ROWS = 8

HGRN_HEADS = 4
HGRN_DK = 128
SSD_HEADS = 8
SSD_HEADDIM = 64
SSD_GROUPS = 2
SSD_STATE = 128
XATTN_HEADS = 4

VMEM_LIMIT_BYTES = 56 * 1024 * 1024


def _rms(x):
    return x * lax.rsqrt(jnp.mean(x * x, axis=-1, keepdims=True) + EPS)


def _sigmoid(x):
    return 1.0 / (1.0 + jnp.exp(-x))


def _silu(x):
    return x * _sigmoid(x)


def _dot(a, b):
    return jnp.dot(a, b, preferred_element_type=F32)


def _dot_nt(a, b):
    return lax.dot_general(a, b, (((1,), (1,)), ((), ())), preferred_element_type=F32)


def _dot_tn(a, b):
    return lax.dot_general(a, b, (((0,), (0,)), ((), ())), preferred_element_type=F32)


def _split3(a):
    a0 = a.astype(BF16)
    r1 = a - a0.astype(F32)
    a1 = r1.astype(BF16)
    a2 = (r1 - a1.astype(F32)).astype(BF16)
    return a0, a1, a2


def _dot01(m01, a):
    a0, a1, a2 = _split3(a)
    return _dot(m01, a0) + _dot(m01, a1) + _dot(m01, a2)


def _resident(shape):
    nd = len(shape)
    return pl.BlockSpec(shape, lambda *_: (0,) * nd, pipeline_mode=pl.Buffered(1))


def _layer(shape, layer):
    nd = len(shape)
    return pl.BlockSpec((None,) + tuple(shape), lambda *_: (layer,) + (0,) * nd, pipeline_mode=pl.Buffered(1))


def _rows3(a):
    return a.reshape(a.shape[0], 1, a.shape[-1])


def _params(n_axes):
    return pltpu.CompilerParams(dimension_semantics=("arbitrary",) * n_axes,
                                vmem_limit_bytes=VMEM_LIMIT_BYTES)


def _kv_body(mem_ref, mw_ref, wk_ref, wv_ref, k_ref, v_ref):
    mn = (_rms(mem_ref[...]) * mw_ref[...]).astype(BF16)
    k_ref[0] = _dot(mn, wk_ref[0].astype(BF16)).astype(BF16)
    v_ref[0] = _dot(mn, wv_ref[0].astype(BF16)).astype(BF16)


def _memory_kv(mem, mem_norm_w, wk, wv):
    depth, d, _ = wk.shape
    b, m, _ = mem.shape
    out = jax.ShapeDtypeStruct((depth, b * m, d), BF16)
    k_all, v_all = pl.pallas_call(
        _kv_body,
        grid=(depth,),
        in_specs=[
            _resident((b * m, d)),
            _resident((1, d)),
            pl.BlockSpec((1, d, d), lambda l: (l, 0, 0)),
            pl.BlockSpec((1, d, d), lambda l: (l, 0, 0)),
        ],
        out_specs=[
            pl.BlockSpec((1, b * m, d), lambda l: (l, 0, 0)),
            pl.BlockSpec((1, b * m, d), lambda l: (l, 0, 0)),
        ],
        out_shape=[out, out],
        compiler_params=_params(1),
        name="memory_kv",
    )(mem.reshape(b * m, d), mem_norm_w.reshape(1, d), wk, wv)
    return k_all.reshape(depth, b, m, d), v_all.reshape(depth, b, m, d)


def _mlp_body(x_ref, nw_ref, w1_ref, w2_ref, fw_ref, o_ref, *, ff_chunk, final):
    x = x_ref[...]
    hb = (_rms(x) * nw_ref[...]).astype(BF16)
    acc = x
    for c in range(w1_ref.shape[1] // ff_chunk):
        a = _dot(hb, w1_ref[:, c * ff_chunk:(c + 1) * ff_chunk])
        a = jnp.square(jnp.maximum(a, 0.0)).astype(BF16)
        acc = acc + _dot(a, w2_ref[c * ff_chunk:(c + 1) * ff_chunk, :])
    if final:
        acc = _rms(acc) * fw_ref[...]
    o_ref[...] = acc


def _mlp(x2, norm_w, w1, w2, final_w, layer, *, final, tb=1024, ff_chunk=1024):
    n, d = x2.shape
    dff = w1.shape[2]
    return pl.pallas_call(
        functools.partial(_mlp_body, ff_chunk=ff_chunk, final=final),
        grid=(n // tb,),
        in_specs=[
            pl.BlockSpec((tb, d), lambda i: (i, 0)),
            _layer((1, d), layer),
            _layer((d, dff), layer),
            _layer((dff, d), layer),
            _resident((1, d)),
        ],
        out_specs=pl.BlockSpec((tb, d), lambda i: (i, 0)),
        out_shape=jax.ShapeDtypeStruct((n, d), F32),
        compiler_params=_params(1),
        name="sq_relu_mlp",
    )(x2, norm_w, w1, w2, final_w.reshape(1, d))


def _attn_body(x_ref, nw_ref, wq_ref, k_ref, v_ref, wo_ref, o_ref, *, heads):
    x = x_ref[0]
    d = x.shape[-1]
    hd = d // heads
    hb = (_rms(x) * nw_ref[...]).astype(BF16)
    q = (_dot(hb, wq_ref[...]) * (hd ** -0.5)).astype(BF16)
    outs = []
    for h in range(heads):
        sl = slice(h * hd, (h + 1) * hd)
        s = _dot_nt(q[:, sl], k_ref[0, 0, :, sl])
        p = jnp.exp(s - jnp.max(s, axis=-1, keepdims=True))
        den = jnp.sum(p, axis=-1, keepdims=True)
        oh = _dot(p.astype(BF16), v_ref[0, 0, :, sl]) * (1.0 / den)
        outs.append(oh.astype(BF16))
    o = jnp.concatenate(outs, axis=-1)
    o_ref[0] = x + _dot(o, wo_ref[...])


def _cross_attention(x, norm_w, wq, k_all, v_all, wo, layer, *, tb=1024):
    b, t, d = x.shape
    m = k_all.shape[2]
    return pl.pallas_call(
        functools.partial(_attn_body, heads=XATTN_HEADS),
        grid=(b, t // tb),
        in_specs=[
            pl.BlockSpec((1, tb, d), lambda i, j: (i, j, 0)),
            _layer((1, d), layer),
            _layer((d, d), layer),
            pl.BlockSpec((1, 1, m, d), lambda i, j: (layer, i, 0, 0)),
            pl.BlockSpec((1, 1, m, d), lambda i, j: (layer, i, 0, 0)),
            _layer((d, d), layer),
        ],
        out_specs=pl.BlockSpec((1, tb, d), lambda i, j: (i, j, 0)),
        out_shape=jax.ShapeDtypeStruct((b, t, d), F32),
        compiler_params=_params(2),
        name="memory_xattn",
    )(x, norm_w, wq, k_all, v_all, wo)


def _slab_conv(u_ref, c_ref, w_taps, bias, *, hist, n_t, t_step, t_lo=None):
    taps = len(w_taps)
    n_slab = bias.shape[0]
    p_u = u_ref.shape[0] // n_slab
    p_c = c_ref.shape[0] // n_slab

    def step(i, carry):
        t0 = i * t_step if isinstance(i, int) else pl.multiple_of(i * t_step, t_step)
        accs = [bias] * t_step
        for k in range(t_step + taps - 1):
            uk = u_ref[pl.ds(t0 + hist - (taps - 1) + k, n_slab, stride=p_u), :]
            for tt in range(t_step):
                if 0 <= k - tt < taps:
                    accs[tt] = accs[tt] + w_taps[k - tt] * uk
        for tt in range(t_step):
            c_ref[pl.ds(t0 + tt, n_slab, stride=p_c), :] = accs[tt]
        return carry

    if t_lo is None:
        lax.fori_loop(0, n_t // t_step, step, 0)
    else:
        for i in range(t_lo // t_step, (t_lo + n_t) // t_step):
            step(i, 0)


def _slab_pitches(hist, n_t):
    return hist + n_t + 4, n_t + SUBLANES


def _conf_body(x_ref, nw_ref, w1_ref, b1_ref, wdw_ref, bdw_ref, lnw_ref, lnb_ref, w2_ref, b2_ref,
               o_ref, u_ref, c_ref, *, taps, hist, t_step):
    tb, d = x_ref.shape[1], x_ref.shape[2]
    n_slab = d // LANES
    p_u = u_ref.shape[0] // n_slab
    p_c = c_ref.shape[0] // n_slab

    @pl.when(pl.program_id(1) == 0)
    def _():
        for s in range(n_slab):
            u_ref[s * p_u:s * p_u + hist, :] = jnp.zeros((hist, LANES), F32)

    half = tb // 2
    w_taps = [wdw_ref[j] for j in range(taps)]

    def glu_to_slabs(ag, lo):
        u = ag[:, :d] * _sigmoid(ag[:, d:])
        for s in range(n_slab):
            u_ref[s * p_u + hist + lo:s * p_u + hist + lo + half, :] = u[:, s * LANES:(s + 1) * LANES]

    def finish(xh, lo):
        c = jnp.concatenate([c_ref[s * p_c + lo:s * p_c + lo + half, :] for s in range(n_slab)], axis=-1)
        mu = jnp.mean(c, axis=-1, keepdims=True)
        cc = c - mu
        var = jnp.mean(cc * cc, axis=-1, keepdims=True)
        yn = cc * lax.rsqrt(var + EPS) * lnw_ref[...] + lnb_ref[...]
        ys = _silu(yn).astype(BF16)
        o_ref[0, lo:lo + half, :] = xh + _dot(ys, w2_ref[...]) + b2_ref[...]

    x0 = x_ref[0, 0:half, :]
    x1 = x_ref[0, half:tb, :]
    ag0 = _dot((_rms(x0) * nw_ref[...]).astype(BF16), w1_ref[...]) + b1_ref[...]
    glu_to_slabs(ag0, 0)
    ag1 = _dot((_rms(x1) * nw_ref[...]).astype(BF16), w1_ref[...]) + b1_ref[...]
    _slab_conv(u_ref, c_ref, w_taps, bdw_ref[...], hist=hist, n_t=half, t_step=t_step, t_lo=0)
    glu_to_slabs(ag1, half)
    finish(x0, 0)
    _slab_conv(u_ref, c_ref, w_taps, bdw_ref[...], hist=hist, n_t=half, t_step=t_step, t_lo=half)
    for s in range(n_slab):
        u_ref[s * p_u:s * p_u + hist, :] = u_ref[s * p_u + tb:s * p_u + tb + hist, :]
    finish(x1, half)


def _conformer(x, norm_w, w_pw1, b_pw1, w_dw, b_dw, ln_w, ln_b, w_pw2, b_pw2, layer, o, *, tb=1024, t_step=16):
    b, t, d = x.shape
    taps = w_dw.shape[1]
    n_slab = d // LANES
    assert n_slab == SUBLANES
    hist = -(-(taps - 1) // SUBLANES) * SUBLANES
    p_u, p_c = _slab_pitches(hist, tb)
    return pl.pallas_call(
        functools.partial(_conf_body, taps=taps, hist=hist, t_step=t_step),
        grid=(b, t // tb),
        in_specs=[
            pl.BlockSpec((1, tb, d), lambda i, j: (i, j, 0)),
            _layer((1, d), layer),
            _layer((d, 2 * d), o),
            _layer((1, 2 * d), o),
            _layer((taps, n_slab, LANES), o),
            _layer((n_slab, LANES), o),
            _layer((1, d), o),
            _layer((1, d), o),
            _layer((d, d), o),
            _layer((1, d), o),
        ],
        out_specs=pl.BlockSpec((1, tb, d), lambda i, j: (i, j, 0)),
        out_shape=jax.ShapeDtypeStruct((b, t, d), F32),
        scratch_shapes=[pltpu.VMEM((n_slab * p_u, LANES), F32), pltpu.VMEM((n_slab * p_c, LANES), F32)],
        compiler_params=_params(2),
        name="conformer_conv",
    )(x, norm_w, w_pw1, _rows3(b_pw1), w_dw.reshape(-1, taps, n_slab, LANES), b_dw.reshape(-1, n_slab, LANES),
      _rows3(ln_w), _rows3(ln_b), w_pw2, _rows3(b_pw2))


def _iota2(shape, axis):
    return lax.broadcasted_iota(jnp.int32, shape, axis)


HGRN_GROUP = SUB * SUBLANES
BLOCK_PITCH = SUB + 4
GROUP_PITCH = SUBLANES * BLOCK_PITCH
LOG2E = 1.4426950408889634
LOCAL_UNROLL = 8


def _mix_body(x_ref, nw_ref, win_ref, lbl_ref, onw_ref, cw_ref, cb_ref, dtb_ref, alog_ref, dsk_ref,
              snw_ref, wout_ref, o_ref,
              pr_ref, qs_ref, ks_ref, bs_ref, vs_ref, od_ref, qd_ref, kd_ref, bd_ref, qx_ref, kx_ref,
              su_ref, sc_ref, xbc_ref, xc_ref, xe_ref, ecs_ref, cs_ref, csr_ref, yd_ref, oc_ref, sth_ref, sts_ref, y_ref, *, e, t_step):
    tb, d = x_ref.shape[1], x_ref.shape[2]
    hk = HGRN_HEADS * HGRN_DK
    sd = SSD_HEADS * SSD_HEADDIM
    gn = SSD_GROUPS * SSD_STATE
    c_q, c_f, c_i, c_g = 0, hk, 2 * hk, 3 * hk
    c_z = 4 * hk
    c_x = c_z + sd
    c_dt = c_x + sd + 2 * gn
    hp = SSD_HEADS // SSD_GROUPS * SSD_HEADDIM
    taps = cw_ref.shape[0]
    hist = PAD_ROWS
    n_slab = cb_ref.shape[0]
    p_u = su_ref.shape[0] // n_slab
    p_c = sc_ref.shape[0] // n_slab
    n_groups = tb // HGRN_GROUP
    n_chunks = tb // CHUNK

    @pl.when(pl.program_id(1) == 0)
    def _():
        for s in range(n_slab):
            su_ref[s * p_u:s * p_u + hist, :] = jnp.zeros((hist, LANES), F32)
        sth_ref[...] = jnp.zeros(sth_ref.shape, F32)
        sts_ref[...] = jnp.zeros(sts_ref.shape, F32)

    x = x_ref[0]
    hb = (_rms(x) * nw_ref[...]).astype(BF16)
    pr_ref[:, 0:c_g] = _dot(hb, win_ref[:, 0:c_g])

    lg = lbl_ref[...]
    ex = jnp.exp(lg - jnp.max(lg, axis=0, keepdims=True))
    probs = ex / jnp.sum(ex, axis=0, keepdims=True)
    lb = jnp.sum(probs[0:e + 1], axis=0, keepdims=True) - probs[0:1]

    rg = _iota2((HGRN_GROUP, HGRN_GROUP), 0)
    cg = _iota2((HGRN_GROUP, HGRN_GROUP), 1)
    same_chunk = (rg // CHUNK) == (cg // CHUNK)
    m01 = lambda cond: jnp.where(same_chunk & cond, 1.0, 0.0).astype(BF16)
    tri_incl = m01(cg <= rg)
    r64 = _iota2((CHUNK, CHUNK), 0)
    c64 = _iota2((CHUNK, CHUNK), 1)
    tri64 = jnp.where(c64 <= r64, 1.0, 0.0).astype(BF16)
    lane2 = _iota2((CHUNK, LANES), 1)
    causal2 = (lane2 % CHUNK) <= _iota2((CHUNK, LANES), 0)
    rb = _iota2((tb, tb), 0)
    cb_ = _iota2((tb, tb), 1)
    tri_rows = jnp.where(((rb // CHUNK) == (cb_ // CHUNK)) & (rb <= cb_), 1.0, 0.0).astype(BF16)
    n_pair = SSD_HEADS // 2
    er = _iota2((LANES, sd), 0)
    expand = jnp.where(_iota2((LANES, sd), 1) // SSD_HEADDIM == er, 1.0, 0.0).astype(BF16)
    ones_k = jnp.ones((HGRN_DK, HGRN_DK), BF16)
    pair_diag = (_iota2((LANES, LANES), 0) // SSD_HEADDIM) == (_iota2((LANES, LANES), 1) // SSD_HEADDIM)

    for g in range(n_groups):
        rows = slice(g * HGRN_GROUP, (g + 1) * HGRN_GROUP)
        orow = rows
        fr = pr_ref[rows, c_f:c_f + hk]
        lf = jnp.log(lb + (1.0 - lb) * _sigmoid(fr)) * LOG2E
        kk = (1.0 - lb) * _sigmoid(-fr)
        bcs = _dot01(tri_incl, lf)
        nb = CHUNK // SUB
        n_blk = HGRN_GROUP // SUB
        ends = [bcs[(i + 1) * SUB - 1:(i + 1) * SUB, :] for i in range(n_blk)]
        zero_row = jnp.zeros((1, hk), F32)

        def at_block_end(back):
            rows_ = [ends[i - back] if (i % nb) - back >= 0 else zero_row for i in range(n_blk)]
            return jnp.concatenate([jnp.broadcast_to(r_, (SUB, hk)) for r_ in rows_], axis=0)

        btot = jnp.concatenate([jnp.broadcast_to(ends[(i // nb) * nb + nb - 1], (SUB, hk)) for i in range(n_blk)], axis=0)
        qq = pr_ref[rows, c_q:c_q + hk] * (HGRN_DK ** -0.5)
        vv = pr_ref[rows, c_i:c_i + hk]
        qd_ref[orow, :] = (qq * jnp.exp2(bcs)).astype(BF16)
        kd_ref[orow, :] = (kk * jnp.exp2(btot - bcs)).astype(BF16)
        bd_ref[orow, :] = jnp.exp2(btot)
        kx_ref[orow, :] = (kk * jnp.exp2(at_block_end(0) - bcs)).astype(BF16)
        for m in range(1, nb):
            qx_ref[m - 1, orow, :] = (qq * jnp.exp2(bcs - at_block_end(m))).astype(BF16)
        for h in range(HGRN_HEADS):
            hs = slice(h * HGRN_DK, (h + 1) * HGRN_DK)
            for blk in range(SUBLANES):
                src = slice(blk * SUB, (blk + 1) * SUB)
                lo = (h * n_groups + g) * GROUP_PITCH + blk * BLOCK_PITCH
                qs_ref[lo:lo + SUB, :] = qq[src, hs]
                ks_ref[lo:lo + SUB, :] = kk[src, hs]
                bs_ref[lo:lo + SUB, :] = bcs[src, hs]
                vs_ref[lo:lo + SUB, :] = vv[src, hs]

    pr_ref[:, c_g:] = _dot(hb, win_ref[:, c_g:])

    def pair_step(idx, carry):
        sb = pl.multiple_of(idx * GROUP_PITCH, SUBLANES)
        ob = pl.multiple_of(idx * HGRN_GROUP, HGRN_GROUP)
        ld = lambda ref, r: ref[pl.ds(sb + r, SUBLANES, stride=BLOCK_PITCH), :]
        qv = [ld(qs_ref, r) for r in range(SUB)]
        kv = [ld(ks_ref, r) for r in range(SUB)]
        bv = [ld(bs_ref, r) for r in range(SUB)]
        vv_ = [ld(vs_ref, r) for r in range(SUB)]
        prods = []
        for r in range(SUB):
            for s_ in range(r + 1):
                if s_ == r:
                    prods.append(qv[r] * kv[r])
                else:
                    prods.append(qv[r] * kv[s_] * jnp.exp2(bv[r] - bv[s_]))
        sc = _dot(jnp.concatenate(prods, axis=0).astype(BF16), ones_k)
        n = 0
        for r in range(SUB):
            acc = None
            for s_ in range(r + 1):
                term = sc[n * SUBLANES:(n + 1) * SUBLANES, :] * vv_[s_]
                acc = term if acc is None else acc + term
                n += 1
            od_ref[pl.ds(ob + r, SUBLANES, stride=SUB), :] = acc
        return carry

    lax.fori_loop(0, HGRN_HEADS * n_groups, pair_step, 0, unroll=4)

    for s in range(n_slab):
        su_ref[s * p_u + hist:s * p_u + hist + tb, :] = pr_ref[:, c_x + s * LANES:c_x + (s + 1) * LANES]
    _slab_conv(su_ref, sc_ref, [cw_ref[j] for j in range(taps)], cb_ref[...], hist=hist, n_t=tb, t_step=t_step)
    for s in range(n_slab):
        su_ref[s * p_u:s * p_u + hist, :] = su_ref[s * p_u + tb:s * p_u + tb + hist, :]
    xbc_ref[...] = _silu(jnp.concatenate([sc_ref[s * p_c:s * p_c + tb, :] for s in range(n_slab)], axis=-1))
    dt = jnp.logaddexp(pr_ref[:, c_dt:c_dt + LANES] + dtb_ref[...], 0.0)
    da = dt * (-jnp.exp(alog_ref[...]))
    d0, d1, d2 = _split3(da)
    for c in range(n_chunks):
        cr = slice(c * CHUNK, (c + 1) * CHUNK)
        cs_ref[cr, :] = _dot(tri64, d0[cr]) + _dot(tri64, d1[cr]) + _dot(tri64, d2[cr])
    cs = cs_ref[...]
    cs_end = jnp.concatenate(
        [jnp.broadcast_to(cs[(c + 1) * CHUNK - 1:(c + 1) * CHUNK, :], (CHUNK, LANES)) for c in range(n_chunks)], axis=0)
    cs_t = _dot_tn(d0, tri_rows) + _dot_tn(d1, tri_rows) + _dot_tn(d2, tri_rows)
    fill = [jnp.zeros((SUBLANES - n_pair, tb), F32)]
    cs_even = jnp.concatenate([cs_t[2 * p:2 * p + 1, :] for p in range(n_pair)] + fill, axis=0)
    cs_odd = jnp.concatenate([cs_t[2 * p + 1:2 * p + 2, :] for p in range(n_pair)] + fill, axis=0)
    for c in range(n_chunks):
        cl = slice(c * CHUNK, (c + 1) * CHUNK)
        csr_ref[c] = jnp.concatenate([cs_even[:, cl], cs_odd[:, cl]], axis=1)
    e0, e1, e2 = _split3(jnp.exp(cs))
    spread = _dot(jnp.concatenate([dt.astype(BF16), jnp.exp(cs_end - cs).astype(BF16), e0, e1, e2], axis=0), expand)
    xc = xbc_ref[:, 0:sd] * spread[0:tb]
    xc_ref[...] = xc.astype(BF16)
    xe_ref[...] = (xc * spread[tb:2 * tb]).astype(BF16)
    ecs_ref[...] = spread[2 * tb:3 * tb] + spread[3 * tb:4 * tb] + spread[4 * tb:5 * tb]

    def local_step(it, carry):
        nb = CHUNK // SUB
        lane64 = _iota2((CHUNK, CHUNK), 1)
        chunks = [it * LOCAL_UNROLL + u for u in range(LOCAL_UNROLL)]
        r0s = [pl.multiple_of(c * CHUNK, CHUNK) for c in chunks]
        stage1 = []
        for c, r0 in zip(chunks, r0s):
            rows = pl.ds(r0, CHUNK)
            res = []
            for h in range(HGRN_HEADS):
                hs = slice(h * HGRN_DK, (h + 1) * HGRN_DK)
                lhs = jnp.concatenate(
                    [qx_ref[i - j - 1, pl.ds(r0 + i * SUB, SUB), hs] for j in range(nb - 1) for i in range(j + 1, nb)],
                    axis=0)
                res.append(_dot_nt(lhs, kx_ref[rows, hs]))
            cbm2 = []
            for g in range(SSD_GROUPS):
                bm = xbc_ref[rows, sd + g * SSD_STATE:sd + (g + 1) * SSD_STATE].astype(BF16)
                cm = xbc_ref[rows, sd + gn + g * SSD_STATE:sd + gn + (g + 1) * SSD_STATE].astype(BF16)
                cbm2.append(_dot_nt(cm, jnp.concatenate([bm, bm], axis=0)))
            stage1.append((res, cbm2))
        for (c, r0), (res, cbm2) in zip(zip(chunks, r0s), stage1):
            rows = pl.ds(r0, CHUNK)
            for h in range(HGRN_HEADS):
                vi = pr_ref[rows, c_i + h * HGRN_DK:c_i + (h + 1) * HGRN_DK].astype(BF16)
                sc = jnp.zeros((CHUNK, CHUNK), F32)
                off = 0
                for j in range(nb - 1):
                    n_r = (nb - 1 - j) * SUB
                    col = jnp.concatenate([jnp.zeros((CHUNK - n_r, CHUNK), F32), res[h][off:off + n_r, :]], axis=0)
                    sc = jnp.where((lane64 // SUB) == j, col, sc)
                    off += n_r
                oc_ref[pl.ds(h * tb + r0, CHUNK), :] = _dot(sc.astype(BF16), vi)
            cs_c = cs_ref[rows, :]
            cs_rows = csr_ref[c]
            for p in range(n_pair):
                cs_pair = jnp.where(lane2 < CHUNK, cs_c[:, 2 * p:2 * p + 1], cs_c[:, 2 * p + 1:2 * p + 2])
                dec = jnp.where(causal2, jnp.exp(cs_pair - cs_rows[p:p + 1, :]), 0.0)
                xp = xc_ref[rows, p * LANES:(p + 1) * LANES]
                rhs = jnp.where(pair_diag, jnp.concatenate([xp, xp], axis=0), jnp.zeros((LANES, LANES), BF16))
                g = p // (n_pair // SSD_GROUPS)
                yd_ref[rows, p * LANES:(p + 1) * LANES] = _dot((cbm2[g] * dec).astype(BF16), rhs)
        return carry

    lax.fori_loop(0, n_chunks // LOCAL_UNROLL, local_step, 0)

    def state_step(c, carry):
        r0 = pl.multiple_of(c * CHUNK, CHUNK)
        rows = pl.ds(r0, CHUNK)
        gate = _silu(pr_ref[rows, c_g:c_g + hk])
        for h in range(HGRN_HEADS):
            hs = slice(h * HGRN_DK, (h + 1) * HGRN_DK)
            vi = pr_ref[rows, c_i + h * HGRN_DK:c_i + (h + 1) * HGRN_DK].astype(BF16)
            st = sth_ref[h]
            orow = pl.ds(h * tb + r0, CHUNK)
            o_h = od_ref[orow, :] + oc_ref[orow, :] + _dot_nt(qd_ref[rows, hs], st.astype(BF16))
            sth_ref[h] = st * bd_ref[pl.ds(r0, 1), hs] + _dot_tn(vi, kd_ref[rows, hs])
            o_h = _rms(o_h) * onw_ref[:, hs] * gate[:, hs]
            y_ref[rows, hs] = o_h.astype(BF16)

        xbc = xbc_ref[rows, :]
        ecs = ecs_ref[rows, :]
        xcd = xe_ref[rows, :]
        ys = []
        for g in range(SSD_GROUPS):
            bm = xbc[:, sd + g * SSD_STATE:sd + (g + 1) * SSD_STATE].astype(BF16)
            cm = xbc[:, sd + gn + g * SSD_STATE:sd + gn + (g + 1) * SSD_STATE].astype(BF16)
            gl = slice(g * hp, (g + 1) * hp)
            st = sts_ref[g]
            ys.append(_dot(cm, st.astype(BF16)) * ecs[:, gl])
            sts_ref[g] = st * ecs[CHUNK - 1:CHUNK, gl] + _dot_tn(bm, xcd[:, gl])
        y = jnp.concatenate(ys, axis=-1) + yd_ref[rows, :] + xbc[:, 0:sd] * dsk_ref[...]
        y = y * _silu(pr_ref[rows, c_z:c_z + sd])
        y = jnp.concatenate([_rms(y[:, g * hp:(g + 1) * hp]) for g in range(SSD_GROUPS)], axis=-1) * snw_ref[...]
        y_ref[rows, hk:hk + sd] = y.astype(BF16)
        return carry

    lax.fori_loop(0, n_chunks, state_step, 0, unroll=n_chunks)

    o_ref[0] = x + _dot(y_ref[...], wout_ref[...])


def _mixer_weights(w_in, dt_bias, a_log, d_skip):
    ab_in = w_in.shape[2]
    pw = -(-ab_in // LANES) * LANES
    w_in_p = jnp.pad(w_in, ((0, 0), (0, 0), (0, pw - ab_in))).astype(BF16)
    pad8 = lambda a: _rows3(jnp.pad(a, ((0, 0), (0, LANES - a.shape[1]))))
    return w_in_p, pad8(dt_bias), pad8(a_log), _rows3(jnp.repeat(d_skip, SSD_HEADDIM, axis=1))


def _mixer(x, norm_w, w_in_p, lb_logits, out_norm_w, conv_w, conv_b, dt_bias_p, a_log_p, d_skip_x, ssd_norm_w,
           w_out, layer, e, *, tb=512, t_step=16):
    b, t, d = x.shape
    n_even, hk = lb_logits.shape
    sd = SSD_HEADS * SSD_HEADDIM
    pw = w_in_p.shape[2]
    taps, cc = conv_w.shape[1:]
    n_slab = cc // LANES
    assert tb % HGRN_GROUP == 0 and n_slab == SUBLANES and taps - 1 <= PAD_ROWS
    p_u, p_c = _slab_pitches(PAD_ROWS, tb)
    slab_rows = HGRN_HEADS * (tb // HGRN_GROUP) * GROUP_PITCH
    slab = pltpu.VMEM((slab_rows, LANES), F32)
    return pl.pallas_call(
        functools.partial(_mix_body, e=e, t_step=t_step),
        grid=(b, t // tb),
        in_specs=[
            pl.BlockSpec((1, tb, d), lambda i, j: (i, j, 0)),
            _layer((1, d), layer),
            _layer((d, pw), e),
            _resident((n_even, hk)),
            _layer((1, hk), e),
            _layer((taps, n_slab, LANES), e),
            _layer((n_slab, LANES), e),
            _layer((1, LANES), e),
            _layer((1, LANES), e),
            _layer((1, sd), e),
            _layer((1, sd), e),
            _layer((hk + sd, d), e),
        ],
        out_specs=pl.BlockSpec((1, tb, d), lambda i, j: (i, j, 0)),
        out_shape=jax.ShapeDtypeStruct((b, t, d), F32),
        scratch_shapes=[
            pltpu.VMEM((tb, pw), F32),
            slab, slab, slab, slab,
            pltpu.VMEM((HGRN_HEADS * tb, LANES), F32),
            pltpu.VMEM((tb, hk), BF16),
            pltpu.VMEM((tb, hk), BF16),
            pltpu.VMEM((tb, hk), F32),
            pltpu.VMEM((CHUNK // SUB - 1, tb, hk), BF16),
            pltpu.VMEM((tb, hk), BF16),
            pltpu.VMEM((n_slab * p_u, LANES), F32),
            pltpu.VMEM((n_slab * p_c, LANES), F32),
            pltpu.VMEM((tb, cc), F32),
            pltpu.VMEM((tb, sd), BF16),
            pltpu.VMEM((tb, sd), BF16),
            pltpu.VMEM((tb, sd), F32),
            pltpu.VMEM((tb, LANES), F32),
            pltpu.VMEM((tb // CHUNK, SUBLANES, LANES), F32),
            pltpu.VMEM((tb, sd), F32),
            pltpu.VMEM((HGRN_HEADS * tb, LANES), F32),
            pltpu.VMEM((HGRN_HEADS, HGRN_DK, HGRN_DK), F32),
            pltpu.VMEM((SSD_GROUPS, SSD_STATE, sd // SSD_GROUPS), F32),
            pltpu.VMEM((tb, hk + sd), BF16),
        ],
        compiler_params=_params(2),
        name="hgrn_ssd_mixer",
    )(x, norm_w, w_in_p, lb_logits, _rows3(out_norm_w), conv_w.reshape(-1, taps, n_slab, LANES),
      conv_b.reshape(-1, n_slab, LANES), dt_bias_p, a_log_p, d_skip_x, _rows3(ssd_norm_w), w_out)


def kernel(x, mem, mem_norm_w, norm_mix_w, ab_w_in, hgrn_lb_logits, hgrn_out_norm_w, ssd_conv_w, ssd_conv_b,
           ssd_dt_bias, ssd_a_log, ssd_d, ssd_norm_w, ab_w_out, cv_w_pw1, cv_b_pw1, cv_w_dw, cv_b_dw, cv_ln_w,
           cv_ln_b, cv_w_pw2, cv_b_pw2, norm_xattn_w, xattn_wq, xattn_wk, xattn_wv, xattn_wo, norm_mlp_w,
           mlp_w1, mlp_w2, final_norm_w):
    b, t, d = x.shape
    depth = norm_mix_w.shape[0]
    bf = lambda a: a.astype(BF16)
    w_in_p, dt_bias_p, a_log_p, d_skip_x = _mixer_weights(ab_w_in, ssd_dt_bias, ssd_a_log, ssd_d)
    w_out, w_pw1, w_pw2, wq, wo, w1, w2 = map(bf, (ab_w_out, cv_w_pw1, cv_w_pw2, xattn_wq, xattn_wo, mlp_w1, mlp_w2))
    n_mix, n_att, n_mlp = _rows3(norm_mix_w), _rows3(norm_xattn_w), _rows3(norm_mlp_w)
    k_all, v_all = _memory_kv(mem, mem_norm_w, xattn_wk, xattn_wv)
    for layer in range(depth):
        if layer % 2 == 0:
            e = layer // 2
            x = _mixer(x, n_mix, w_in_p, hgrn_lb_logits, hgrn_out_norm_w, ssd_conv_w, ssd_conv_b, dt_bias_p, a_log_p,
                       d_skip_x, ssd_norm_w, w_out, layer, e)
        else:
            x = _conformer(x, n_mix, w_pw1, cv_b_pw1, cv_w_dw, cv_b_dw, cv_ln_w, cv_ln_b, w_pw2, cv_b_pw2, layer,
                           layer // 2)
        x = _cross_attention(x, n_att, wq, k_all, v_all, wo, layer)
        x = _mlp(x.reshape(b * t, d), n_mlp, w1, w2, final_norm_w, layer,
                 final=(layer == depth - 1)).reshape(b, t, d)
    return x
```

```python
import functools

import jax
import jax.numpy as jnp
from jax import lax
from jax.experimental import pallas as pl
from jax.experimental.pallas import tpu as pltpu

F32 = jnp.float32
BF16 = jnp.bfloat16

EPS = 1e-6
CHUNK = 64
SUB = 16
LANES = 128
SUBLANES = 8
PAD_ROWS = 8

HGRN_HEADS = 4
HGRN_DK = 128
SSD_HEADS = 8
SSD_HEADDIM = 64
SSD_GROUPS = 2
SSD_STATE = 128
XATTN_HEADS = 4

VMEM_LIMIT_BYTES = 56 * 1024 * 1024


def _rms(x):
    return x * lax.rsqrt(jnp.mean(x * x, axis=-1, keepdims=True) + EPS)


def _sigmoid(x):
    return 1.0 / (1.0 + jnp.exp(-x))


def _silu(x):
    return x * _sigmoid(x)


def _dot(a, b):
    return jnp.dot(a, b, preferred_element_type=F32)


def _dot_nt(a, b):
    return lax.dot_general(a, b, (((1,), (1,)), ((), ())), preferred_element_type=F32)


def _dot_tn(a, b):
    return lax.dot_general(a, b, (((0,), (0,)), ((), ())), preferred_element_type=F32)


def _split3(a):
    a0 = a.astype(BF16)
    r1 = a - a0.astype(F32)
    a1 = r1.astype(BF16)
    a2 = (r1 - a1.astype(F32)).astype(BF16)
    return a0, a1, a2


def _dot01(m01, a):
    a0, a1, a2 = _split3(a)
    return _dot(m01, a0) + _dot(m01, a1) + _dot(m01, a2)


def _resident(shape):
    nd = len(shape)
    return pl.BlockSpec(shape, lambda *_: (0,) * nd, pipeline_mode=pl.Buffered(1))


def _layer(shape, layer):
    nd = len(shape)
    return pl.BlockSpec((None,) + tuple(shape), lambda *_: (layer,) + (0,) * nd, pipeline_mode=pl.Buffered(1))


def _rows3(a):
    return a.reshape(a.shape[0], 1, a.shape[-1])


def _params(n_axes):
    return pltpu.CompilerParams(dimension_semantics=("arbitrary",) * n_axes,
                                vmem_limit_bytes=VMEM_LIMIT_BYTES)


def _kv_body(mem_ref, mw_ref, wk_ref, wv_ref, k_ref, v_ref):
    mn = (_rms(mem_ref[...]) * mw_ref[...]).astype(BF16)
    k_ref[0] = _dot(mn, wk_ref[0].astype(BF16)).astype(BF16)
    v_ref[0] = _dot(mn, wv_ref[0].astype(BF16)).astype(BF16)


def _memory_kv(mem, mem_norm_w, wk, wv):
    depth, d, _ = wk.shape
    b, m, _ = mem.shape
    out = jax.ShapeDtypeStruct((depth, b * m, d), BF16)
    k_all, v_all = pl.pallas_call(
        _kv_body,
        grid=(depth,),
        in_specs=[
            _resident((b * m, d)),
            _resident((1, d)),
            pl.BlockSpec((1, d, d), lambda l: (l, 0, 0)),
            pl.BlockSpec((1, d, d), lambda l: (l, 0, 0)),
        ],
        out_specs=[
            pl.BlockSpec((1, b * m, d), lambda l: (l, 0, 0)),
            pl.BlockSpec((1, b * m, d), lambda l: (l, 0, 0)),
        ],
        out_shape=[out, out],
        compiler_params=_params(1),
        name="memory_kv",
    )(mem.reshape(b * m, d), mem_norm_w.reshape(1, d), wk, wv)
    return k_all.reshape(depth, b, m, d), v_all.reshape(depth, b, m, d)


def _mlp_body(x_ref, nw_ref, w1_ref, w2_ref, fw_ref, o_ref, *, ff_chunk, final):
    x = x_ref[...]
    hb = (_rms(x) * nw_ref[...]).astype(BF16)
    acc = x
    for c in range(w1_ref.shape[1] // ff_chunk):
        a = _dot(hb, w1_ref[:, c * ff_chunk:(c + 1) * ff_chunk])
        a = jnp.square(jnp.maximum(a, 0.0)).astype(BF16)
        acc = acc + _dot(a, w2_ref[c * ff_chunk:(c + 1) * ff_chunk, :])
    if final:
        acc = _rms(acc) * fw_ref[...]
    o_ref[...] = acc


def _mlp(x2, norm_w, w1, w2, final_w, layer, *, final, tb=1024, ff_chunk=1024):
    n, d = x2.shape
    dff = w1.shape[2]
    return pl.pallas_call(
        functools.partial(_mlp_body, ff_chunk=ff_chunk, final=final),
        grid=(n // tb,),
        in_specs=[
            pl.BlockSpec((tb, d), lambda i: (i, 0)),
            _layer((1, d), layer),
            _layer((d, dff), layer),
            _layer((dff, d), layer),
            _resident((1, d)),
        ],
        out_specs=pl.BlockSpec((tb, d), lambda i: (i, 0)),
        out_shape=jax.ShapeDtypeStruct((n, d), F32),
        compiler_params=_params(1),
        name="sq_relu_mlp",
    )(x2, norm_w, w1, w2, final_w.reshape(1, d))


def _attn_body(x_ref, nw_ref, wq_ref, k_ref, v_ref, wo_ref, o_ref, *, heads):
    x = x_ref[0]
    d = x.shape[-1]
    hd = d // heads
    hb = (_rms(x) * nw_ref[...]).astype(BF16)
    q = (_dot(hb, wq_ref[...]) * (hd ** -0.5)).astype(BF16)
    outs = []
    for h in range(heads):
        sl = slice(h * hd, (h + 1) * hd)
        s = _dot_nt(q[:, sl], k_ref[0, 0, :, sl])
        p = jnp.exp(s - jnp.max(s, axis=-1, keepdims=True))
        den = jnp.sum(p, axis=-1, keepdims=True)
        oh = _dot(p.astype(BF16), v_ref[0, 0, :, sl]) * (1.0 / den)
        outs.append(oh.astype(BF16))
    o = jnp.concatenate(outs, axis=-1)
    o_ref[0] = x + _dot(o, wo_ref[...])


def _cross_attention(x, norm_w, wq, k_all, v_all, wo, layer, *, tb=1024):
    b, t, d = x.shape
    m = k_all.shape[2]
    return pl.pallas_call(
        functools.partial(_attn_body, heads=XATTN_HEADS),
        grid=(b, t // tb),
        in_specs=[
            pl.BlockSpec((1, tb, d), lambda i, j: (i, j, 0)),
            _layer((1, d), layer),
            _layer((d, d), layer),
            pl.BlockSpec((1, 1, m, d), lambda i, j: (layer, i, 0, 0)),
            pl.BlockSpec((1, 1, m, d), lambda i, j: (layer, i, 0, 0)),
            _layer((d, d), layer),
        ],
        out_specs=pl.BlockSpec((1, tb, d), lambda i, j: (i, j, 0)),
        out_shape=jax.ShapeDtypeStruct((b, t, d), F32),
        compiler_params=_params(2),
        name="memory_xattn",
    )(x, norm_w, wq, k_all, v_all, wo)


def _slab_conv(u_ref, c_ref, w_taps, bias, *, hist, n_t, t_step, t_lo=None):
    taps = len(w_taps)
    n_slab = bias.shape[0]
    p_u = u_ref.shape[0] // n_slab
    p_c = c_ref.shape[0] // n_slab

    def step(i, carry):
        t0 = i * t_step if isinstance(i, int) else pl.multiple_of(i * t_step, t_step)
        accs = [bias] * t_step
        for k in range(t_step + taps - 1):
            uk = u_ref[pl.ds(t0 + hist - (taps - 1) + k, n_slab, stride=p_u), :]
            for tt in range(t_step):
                if 0 <= k - tt < taps:
                    accs[tt] = accs[tt] + w_taps[k - tt] * uk
        for tt in range(t_step):
            c_ref[pl.ds(t0 + tt, n_slab, stride=p_c), :] = accs[tt]
        return carry

    if t_lo is None:
        lax.fori_loop(0, n_t // t_step, step, 0)
    else:
        for i in range(t_lo // t_step, (t_lo + n_t) // t_step):
            step(i, 0)


def _slab_pitches(hist, n_t):
    return hist + n_t + 4, n_t + SUBLANES


def _conf_body(x_ref, nw_ref, w1_ref, b1_ref, wdw_ref, bdw_ref, lnw_ref, lnb_ref, w2_ref, b2_ref,
               o_ref, u_ref, c_ref, *, taps, hist, t_step):
    tb, d = x_ref.shape[1], x_ref.shape[2]
    n_slab = d // LANES
    p_u = u_ref.shape[0] // n_slab
    p_c = c_ref.shape[0] // n_slab

    @pl.when(pl.program_id(1) == 0)
    def _():
        for s in range(n_slab):
            u_ref[s * p_u:s * p_u + hist, :] = jnp.zeros((hist, LANES), F32)

    half = tb // 2
    w_taps = [wdw_ref[j] for j in range(taps)]

    def glu_to_slabs(ag, lo):
        u = ag[:, :d] * _sigmoid(ag[:, d:])
        for s in range(n_slab):
            u_ref[s * p_u + hist + lo:s * p_u + hist + lo + half, :] = u[:, s * LANES:(s + 1) * LANES]

    def finish(xh, lo):
        c = jnp.concatenate([c_ref[s * p_c + lo:s * p_c + lo + half, :] for s in range(n_slab)], axis=-1)
        mu = jnp.mean(c, axis=-1, keepdims=True)
        cc = c - mu
        var = jnp.mean(cc * cc, axis=-1, keepdims=True)
        yn = cc * lax.rsqrt(var + EPS) * lnw_ref[...] + lnb_ref[...]
        ys = _silu(yn).astype(BF16)
        o_ref[0, lo:lo + half, :] = xh + _dot(ys, w2_ref[...]) + b2_ref[...]

    x0 = x_ref[0, 0:half, :]
    x1 = x_ref[0, half:tb, :]
    ag0 = _dot((_rms(x0) * nw_ref[...]).astype(BF16), w1_ref[...]) + b1_ref[...]
    glu_to_slabs(ag0, 0)
    ag1 = _dot((_rms(x1) * nw_ref[...]).astype(BF16), w1_ref[...]) + b1_ref[...]
    _slab_conv(u_ref, c_ref, w_taps, bdw_ref[...], hist=hist, n_t=half, t_step=t_step, t_lo=0)
    glu_to_slabs(ag1, half)
    finish(x0, 0)
    _slab_conv(u_ref, c_ref, w_taps, bdw_ref[...], hist=hist, n_t=half, t_step=t_step, t_lo=half)
    for s in range(n_slab):
        u_ref[s * p_u:s * p_u + hist, :] = u_ref[s * p_u + tb:s * p_u + tb + hist, :]
    finish(x1, half)


def _conformer(x, norm_w, w_pw1, b_pw1, w_dw, b_dw, ln_w, ln_b, w_pw2, b_pw2, layer, o, *, tb=1024, t_step=16):
    b, t, d = x.shape
    taps = w_dw.shape[1]
    n_slab = d // LANES
    assert n_slab == SUBLANES
    hist = -(-(taps - 1) // SUBLANES) * SUBLANES
    p_u, p_c = _slab_pitches(hist, tb)
    return pl.pallas_call(
        functools.partial(_conf_body, taps=taps, hist=hist, t_step=t_step),
        grid=(b, t // tb),
        in_specs=[
            pl.BlockSpec((1, tb, d), lambda i, j: (i, j, 0)),
            _layer((1, d), layer),
            _layer((d, 2 * d), o),
            _layer((1, 2 * d), o),
            _layer((taps, n_slab, LANES), o),
            _layer((n_slab, LANES), o),
            _layer((1, d), o),
            _layer((1, d), o),
            _layer((d, d), o),
            _layer((1, d), o),
        ],
        out_specs=pl.BlockSpec((1, tb, d), lambda i, j: (i, j, 0)),
        out_shape=jax.ShapeDtypeStruct((b, t, d), F32),
        scratch_shapes=[pltpu.VMEM((n_slab * p_u, LANES), F32), pltpu.VMEM((n_slab * p_c, LANES), F32)],
        compiler_params=_params(2),
        name="conformer_conv",
    )(x, norm_w, w_pw1, _rows3(b_pw1), w_dw.reshape(-1, taps, n_slab, LANES), b_dw.reshape(-1, n_slab, LANES),
      _rows3(ln_w), _rows3(ln_b), w_pw2, _rows3(b_pw2))


def _iota2(shape, axis):
    return lax.broadcasted_iota(jnp.int32, shape, axis)


HGRN_GROUP = SUB * SUBLANES
BLOCK_PITCH = SUB + 4
GROUP_PITCH = SUBLANES * BLOCK_PITCH
LOG2E = 1.4426950408889634
LOCAL_UNROLL = 8


def _mix_body(x_ref, nw_ref, win_ref, lbl_ref, onw_ref, cw_ref, cb_ref, dtb_ref, alog_ref, dsk_ref,
              snw_ref, wout_ref, o_ref,
              pr_ref, qs_ref, ks_ref, bs_ref, vs_ref, od_ref, qd_ref, kd_ref, bd_ref, qx_ref, kx_ref,
              su_ref, sc_ref, xbc_ref, xc_ref, xe_ref, ecs_ref, cs_ref, csr_ref, yd_ref, oc_ref, sth_ref, sts_ref, y_ref, *, e, t_step):
    tb, d = x_ref.shape[1], x_ref.shape[2]
    hk = HGRN_HEADS * HGRN_DK
    sd = SSD_HEADS * SSD_HEADDIM
    gn = SSD_GROUPS * SSD_STATE
    c_q, c_f, c_i, c_g = 0, hk, 2 * hk, 3 * hk
    c_z = 4 * hk
    c_x = c_z + sd
    c_dt = c_x + sd + 2 * gn
    hp = SSD_HEADS // SSD_GROUPS * SSD_HEADDIM
    taps = cw_ref.shape[0]
    hist = PAD_ROWS
    n_slab = cb_ref.shape[0]
    p_u = su_ref.shape[0] // n_slab
    p_c = sc_ref.shape[0] // n_slab
    n_groups = tb // HGRN_GROUP
    n_chunks = tb // CHUNK

    @pl.when(pl.program_id(1) == 0)
    def _():
        for s in range(n_slab):
            su_ref[s * p_u:s * p_u + hist, :] = jnp.zeros((hist, LANES), F32)
        sth_ref[...] = jnp.zeros(sth_ref.shape, F32)
        sts_ref[...] = jnp.zeros(sts_ref.shape, F32)

    x = x_ref[0]
    hb = (_rms(x) * nw_ref[...]).astype(BF16)
    pr_ref[:, 0:c_g] = _dot(hb, win_ref[:, 0:c_g])

    lg = lbl_ref[...]
    ex = jnp.exp(lg - jnp.max(lg, axis=0, keepdims=True))
    probs = ex / jnp.sum(ex, axis=0, keepdims=True)
    lb = jnp.sum(probs[0:e + 1], axis=0, keepdims=True) - probs[0:1]

    rg = _iota2((HGRN_GROUP, HGRN_GROUP), 0)
    cg = _iota2((HGRN_GROUP, HGRN_GROUP), 1)
    same_chunk = (rg // CHUNK) == (cg // CHUNK)
    m01 = lambda cond: jnp.where(same_chunk & cond, 1.0, 0.0).astype(BF16)
    tri_incl = m01(cg <= rg)
    r64 = _iota2((CHUNK, CHUNK), 0)
    c64 = _iota2((CHUNK, CHUNK), 1)
    tri64 = jnp.where(c64 <= r64, 1.0, 0.0).astype(BF16)
    lane2 = _iota2((CHUNK, LANES), 1)
    causal2 = (lane2 % CHUNK) <= _iota2((CHUNK, LANES), 0)
    rb = _iota2((tb, tb), 0)
    cb_ = _iota2((tb, tb), 1)
    tri_rows = jnp.where(((rb // CHUNK) == (cb_ // CHUNK)) & (rb <= cb_), 1.0, 0.0).astype(BF16)
    n_pair = SSD_HEADS // 2
    er = _iota2((LANES, sd), 0)
    expand = jnp.where(_iota2((LANES, sd), 1) // SSD_HEADDIM == er, 1.0, 0.0).astype(BF16)
    ones_k = jnp.ones((HGRN_DK, HGRN_DK), BF16)
    pair_diag = (_iota2((LANES, LANES), 0) // SSD_HEADDIM) == (_iota2((LANES, LANES), 1) // SSD_HEADDIM)

    for g in range(n_groups):
        rows = slice(g * HGRN_GROUP, (g + 1) * HGRN_GROUP)
        orow = rows
        fr = pr_ref[rows, c_f:c_f + hk]
        lf = jnp.log(lb + (1.0 - lb) * _sigmoid(fr)) * LOG2E
        kk = (1.0 - lb) * _sigmoid(-fr)
        bcs = _dot01(tri_incl, lf)
        nb = CHUNK // SUB
        n_blk = HGRN_GROUP // SUB
        ends = [bcs[(i + 1) * SUB - 1:(i + 1) * SUB, :] for i in range(n_blk)]
        zero_row = jnp.zeros((1, hk), F32)

        def at_block_end(back):
            rows_ = [ends[i - back] if (i % nb) - back >= 0 else zero_row for i in range(n_blk)]
            return jnp.concatenate([jnp.broadcast_to(r_, (SUB, hk)) for r_ in rows_], axis=0)

        btot = jnp.concatenate([jnp.broadcast_to(ends[(i // nb) * nb + nb - 1], (SUB, hk)) for i in range(n_blk)], axis=0)
        qq = pr_ref[rows, c_q:c_q + hk] * (HGRN_DK ** -0.5)
        vv = pr_ref[rows, c_i:c_i + hk]
        qd_ref[orow, :] = (qq * jnp.exp2(bcs)).astype(BF16)
        kd_ref[orow, :] = (kk * jnp.exp2(btot - bcs)).astype(BF16)
        bd_ref[orow, :] = jnp.exp2(btot)
        kx_ref[orow, :] = (kk * jnp.exp2(at_block_end(0) - bcs)).astype(BF16)
        for m in range(1, nb):
            qx_ref[m - 1, orow, :] = (qq * jnp.exp2(bcs - at_block_end(m))).astype(BF16)
        for h in range(HGRN_HEADS):
            hs = slice(h * HGRN_DK, (h + 1) * HGRN_DK)
            for blk in range(SUBLANES):
                src = slice(blk * SUB, (blk + 1) * SUB)
                lo = (h * n_groups + g) * GROUP_PITCH + blk * BLOCK_PITCH
                qs_ref[lo:lo + SUB, :] = qq[src, hs]
                ks_ref[lo:lo + SUB, :] = kk[src, hs]
                bs_ref[lo:lo + SUB, :] = bcs[src, hs]
                vs_ref[lo:lo + SUB, :] = vv[src, hs]

    pr_ref[:, c_g:] = _dot(hb, win_ref[:, c_g:])

    def pair_step(idx, carry):
        sb = pl.multiple_of(idx * GROUP_PITCH, SUBLANES)
        ob = pl.multiple_of(idx * HGRN_GROUP, HGRN_GROUP)
        ld = lambda ref, r: ref[pl.ds(sb + r, SUBLANES, stride=BLOCK_PITCH), :]
        qv = [ld(qs_ref, r) for r in range(SUB)]
        kv = [ld(ks_ref, r) for r in range(SUB)]
        bv = [ld(bs_ref, r) for r in range(SUB)]
        vv_ = [ld(vs_ref, r) for r in range(SUB)]
        prods = []
        for r in range(SUB):
            for s_ in range(r + 1):
                if s_ == r:
                    prods.append(qv[r] * kv[r])
                else:
                    prods.append(qv[r] * kv[s_] * jnp.exp2(bv[r] - bv[s_]))
        sc = _dot(jnp.concatenate(prods, axis=0).astype(BF16), ones_k)
        n = 0
        for r in range(SUB):
            acc = None
            for s_ in range(r + 1):
                term = sc[n * SUBLANES:(n + 1) * SUBLANES, :] * vv_[s_]
                acc = term if acc is None else acc + term
                n += 1
            od_ref[pl.ds(ob + r, SUBLANES, stride=SUB), :] = acc
        return carry

    lax.fori_loop(0, HGRN_HEADS * n_groups, pair_step, 0, unroll=8)

    for s in range(n_slab):
        su_ref[s * p_u + hist:s * p_u + hist + tb, :] = pr_ref[:, c_x + s * LANES:c_x + (s + 1) * LANES]
    _slab_conv(su_ref, sc_ref, [cw_ref[j] for j in range(taps)], cb_ref[...], hist=hist, n_t=tb, t_step=t_step)
    for s in range(n_slab):
        su_ref[s * p_u:s * p_u + hist, :] = su_ref[s * p_u + tb:s * p_u + tb + hist, :]
    xbc_ref[...] = _silu(jnp.concatenate([sc_ref[s * p_c:s * p_c + tb, :] for s in range(n_slab)], axis=-1))
    dt = jnp.logaddexp(pr_ref[:, c_dt:c_dt + LANES] + dtb_ref[...], 0.0)
    da = dt * (-jnp.exp(alog_ref[...]))
    d0, d1, d2 = _split3(da)
    for c in range(n_chunks):
        cr = slice(c * CHUNK, (c + 1) * CHUNK)
        cs_ref[cr, :] = _dot(tri64, d0[cr]) + _dot(tri64, d1[cr]) + _dot(tri64, d2[cr])
    cs = cs_ref[...]
    cs_end = jnp.concatenate(
        [jnp.broadcast_to(cs[(c + 1) * CHUNK - 1:(c + 1) * CHUNK, :], (CHUNK, LANES)) for c in range(n_chunks)], axis=0)
    cs_t = _dot_tn(d0, tri_rows) + _dot_tn(d1, tri_rows) + _dot_tn(d2, tri_rows)
    fill = [jnp.zeros((SUBLANES - n_pair, tb), F32)]
    cs_even = jnp.concatenate([cs_t[2 * p:2 * p + 1, :] for p in range(n_pair)] + fill, axis=0)
    cs_odd = jnp.concatenate([cs_t[2 * p + 1:2 * p + 2, :] for p in range(n_pair)] + fill, axis=0)
    for c in range(n_chunks):
        cl = slice(c * CHUNK, (c + 1) * CHUNK)
        csr_ref[c] = jnp.concatenate([cs_even[:, cl], cs_odd[:, cl]], axis=1)
    e0, e1, e2 = _split3(jnp.exp(cs))
    spread = _dot(jnp.concatenate([dt.astype(BF16), jnp.exp(cs_end - cs).astype(BF16), e0, e1, e2], axis=0), expand)
    xc = xbc_ref[:, 0:sd] * spread[0:tb]
    xc_ref[...] = xc.astype(BF16)
    xe_ref[...] = (xc * spread[tb:2 * tb]).astype(BF16)
    ecs_ref[...] = spread[2 * tb:3 * tb] + spread[3 * tb:4 * tb] + spread[4 * tb:5 * tb]

    def local_step(it, carry):
        nb = CHUNK // SUB
        lane64 = _iota2((CHUNK, CHUNK), 1)
        chunks = [it * LOCAL_UNROLL + u for u in range(LOCAL_UNROLL)]
        r0s = [pl.multiple_of(c * CHUNK, CHUNK) for c in chunks]
        stage1 = []
        for c, r0 in zip(chunks, r0s):
            rows = pl.ds(r0, CHUNK)
            res = []
            for h in range(HGRN_HEADS):
                hs = slice(h * HGRN_DK, (h + 1) * HGRN_DK)
                lhs = jnp.concatenate(
                    [qx_ref[i - j - 1, pl.ds(r0 + i * SUB, SUB), hs] for j in range(nb - 1) for i in range(j + 1, nb)],
                    axis=0)
                res.append(_dot_nt(lhs, kx_ref[rows, hs]))
            cbm2 = []
            for g in range(SSD_GROUPS):
                bm = xbc_ref[rows, sd + g * SSD_STATE:sd + (g + 1) * SSD_STATE].astype(BF16)
                cm = xbc_ref[rows, sd + gn + g * SSD_STATE:sd + gn + (g + 1) * SSD_STATE].astype(BF16)
                cbm2.append(_dot_nt(cm, jnp.concatenate([bm, bm], axis=0)))
            stage1.append((res, cbm2))
        for (c, r0), (res, cbm2) in zip(zip(chunks, r0s), stage1):
            rows = pl.ds(r0, CHUNK)
            for h in range(HGRN_HEADS):
                vi = pr_ref[rows, c_i + h * HGRN_DK:c_i + (h + 1) * HGRN_DK].astype(BF16)
                sc = jnp.zeros((CHUNK, CHUNK), F32)
                off = 0
                for j in range(nb - 1):
                    n_r = (nb - 1 - j) * SUB
                    col = jnp.concatenate([jnp.zeros((CHUNK - n_r, CHUNK), F32), res[h][off:off + n_r, :]], axis=0)
                    sc = jnp.where((lane64 // SUB) == j, col, sc)
                    off += n_r
                oc_ref[pl.ds(h * tb + r0, CHUNK), :] = _dot(sc.astype(BF16), vi)
            cs_c = cs_ref[rows, :]
            cs_rows = csr_ref[c]
            for p in range(n_pair):
                cs_pair = jnp.where(lane2 < CHUNK, cs_c[:, 2 * p:2 * p + 1], cs_c[:, 2 * p + 1:2 * p + 2])
                dec = jnp.where(causal2, jnp.exp(cs_pair - cs_rows[p:p + 1, :]), 0.0)
                xp = xc_ref[rows, p * LANES:(p + 1) * LANES]
                rhs = jnp.where(pair_diag, jnp.concatenate([xp, xp], axis=0), jnp.zeros((LANES, LANES), BF16))
                g = p // (n_pair // SSD_GROUPS)
                yd_ref[rows, p * LANES:(p + 1) * LANES] = _dot((cbm2[g] * dec).astype(BF16), rhs)
        return carry

    lax.fori_loop(0, n_chunks // LOCAL_UNROLL, local_step, 0)

    def state_step(c, carry):
        r0 = pl.multiple_of(c * CHUNK, CHUNK)
        rows = pl.ds(r0, CHUNK)
        gate = _silu(pr_ref[rows, c_g:c_g + hk])
        for h in range(HGRN_HEADS):
            hs = slice(h * HGRN_DK, (h + 1) * HGRN_DK)
            vi = pr_ref[rows, c_i + h * HGRN_DK:c_i + (h + 1) * HGRN_DK].astype(BF16)
            st = sth_ref[h]
            orow = pl.ds(h * tb + r0, CHUNK)
            o_h = od_ref[orow, :] + oc_ref[orow, :] + _dot_nt(qd_ref[rows, hs], st.astype(BF16))
            sth_ref[h] = st * bd_ref[pl.ds(r0, 1), hs] + _dot_tn(vi, kd_ref[rows, hs])
            o_h = _rms(o_h) * onw_ref[:, hs] * gate[:, hs]
            y_ref[rows, hs] = o_h.astype(BF16)

        xbc = xbc_ref[rows, :]
        ecs = ecs_ref[rows, :]
        xcd = xe_ref[rows, :]
        ys = []
        for g in range(SSD_GROUPS):
            bm = xbc[:, sd + g * SSD_STATE:sd + (g + 1) * SSD_STATE].astype(BF16)
            cm = xbc[:, sd + gn + g * SSD_STATE:sd + gn + (g + 1) * SSD_STATE].astype(BF16)
            gl = slice(g * hp, (g + 1) * hp)
            st = sts_ref[g]
            ys.append(_dot(cm, st.astype(BF16)) * ecs[:, gl])
            sts_ref[g] = st * ecs[CHUNK - 1:CHUNK, gl] + _dot_tn(bm, xcd[:, gl])
        y = jnp.concatenate(ys, axis=-1) + yd_ref[rows, :] + xbc[:, 0:sd] * dsk_ref[...]
        y = y * _silu(pr_ref[rows, c_z:c_z + sd])
        y = jnp.concatenate([_rms(y[:, g * hp:(g + 1) * hp]) for g in range(SSD_GROUPS)], axis=-1) * snw_ref[...]
        y_ref[rows, hk:hk + sd] = y.astype(BF16)
        return carry

    lax.fori_loop(0, n_chunks, state_step, 0, unroll=n_chunks)

    o_ref[0] = x + _dot(y_ref[...], wout_ref[...])


def _mixer_weights(w_in, dt_bias, a_log, d_skip):
    ab_in = w_in.shape[2]
    pw = -(-ab_in // LANES) * LANES
    w_in_p = jnp.pad(w_in, ((0, 0), (0, 0), (0, pw - ab_in))).astype(BF16)
    pad8 = lambda a: _rows3(jnp.pad(a, ((0, 0), (0, LANES - a.shape[1]))))
    return w_in_p, pad8(dt_bias), pad8(a_log), _rows3(jnp.repeat(d_skip, SSD_HEADDIM, axis=1))


def _mixer(x, norm_w, w_in_p, lb_logits, out_norm_w, conv_w, conv_b, dt_bias_p, a_log_p, d_skip_x, ssd_norm_w,
           w_out, layer, e, *, tb=512, t_step=16):
    b, t, d = x.shape
    n_even, hk = lb_logits.shape
    sd = SSD_HEADS * SSD_HEADDIM
    pw = w_in_p.shape[2]
    taps, cc = conv_w.shape[1:]
    n_slab = cc // LANES
    assert tb % HGRN_GROUP == 0 and n_slab == SUBLANES and taps - 1 <= PAD_ROWS
    p_u, p_c = _slab_pitches(PAD_ROWS, tb)
    slab_rows = HGRN_HEADS * (tb // HGRN_GROUP) * GROUP_PITCH
    slab = pltpu.VMEM((slab_rows, LANES), F32)
    return pl.pallas_call(
        functools.partial(_mix_body, e=e, t_step=t_step),
        grid=(b, t // tb),
        in_specs=[
            pl.BlockSpec((1, tb, d), lambda i, j: (i, j, 0)),
            _layer((1, d), layer),
            _layer((d, pw), e),
            _resident((n_even, hk)),
            _layer((1, hk), e),
            _layer((taps, n_slab, LANES), e),
            _layer((n_slab, LANES), e),
            _layer((1, LANES), e),
            _layer((1, LANES), e),
            _layer((1, sd), e),
            _layer((1, sd), e),
            _layer((hk + sd, d), e),
        ],
        out_specs=pl.BlockSpec((1, tb, d), lambda i, j: (i, j, 0)),
        out_shape=jax.ShapeDtypeStruct((b, t, d), F32),
        scratch_shapes=[
            pltpu.VMEM((tb, pw), F32),
            slab, slab, slab, slab,
            pltpu.VMEM((HGRN_HEADS * tb, LANES), F32),
            pltpu.VMEM((tb, hk), BF16),
            pltpu.VMEM((tb, hk), BF16),
            pltpu.VMEM((tb, hk), F32),
            pltpu.VMEM((CHUNK // SUB - 1, tb, hk), BF16),
            pltpu.VMEM((tb, hk), BF16),
            pltpu.VMEM((n_slab * p_u, LANES), F32),
            pltpu.VMEM((n_slab * p_c, LANES), F32),
            pltpu.VMEM((tb, cc), F32),
            pltpu.VMEM((tb, sd), BF16),
            pltpu.VMEM((tb, sd), BF16),
            pltpu.VMEM((tb, sd), F32),
            pltpu.VMEM((tb, LANES), F32),
            pltpu.VMEM((tb // CHUNK, SUBLANES, LANES), F32),
            pltpu.VMEM((tb, sd), F32),
            pltpu.VMEM((HGRN_HEADS * tb, LANES), F32),
            pltpu.VMEM((HGRN_HEADS, HGRN_DK, HGRN_DK), F32),
            pltpu.VMEM((SSD_GROUPS, SSD_STATE, sd // SSD_GROUPS), F32),
            pltpu.VMEM((tb, hk + sd), BF16),
        ],
        compiler_params=_params(2),
        name="hgrn_ssd_mixer",
    )(x, norm_w, w_in_p, lb_logits, _rows3(out_norm_w), conv_w.reshape(-1, taps, n_slab, LANES),
      conv_b.reshape(-1, n_slab, LANES), dt_bias_p, a_log_p, d_skip_x, _rows3(ssd_norm_w), w_out)


def kernel(x, mem, mem_norm_w, norm_mix_w, ab_w_in, hgrn_lb_logits, hgrn_out_norm_w, ssd_conv_w, ssd_conv_b,
           ssd_dt_bias, ssd_a_log, ssd_d, ssd_norm_w, ab_w_out, cv_w_pw1, cv_b_pw1, cv_w_dw, cv_b_dw, cv_ln_w,
           cv_ln_b, cv_w_pw2, cv_b_pw2, norm_xattn_w, xattn_wq, xattn_wk, xattn_wv, xattn_wo, norm_mlp_w,
           mlp_w1, mlp_w2, final_norm_w):
    b, t, d = x.shape
    depth = norm_mix_w.shape[0]
    bf = lambda a: a.astype(BF16)
    w_in_p, dt_bias_p, a_log_p, d_skip_x = _mixer_weights(ab_w_in, ssd_dt_bias, ssd_a_log, ssd_d)
    w_out, w_pw1, w_pw2, wq, wo, w1, w2 = map(bf, (ab_w_out, cv_w_pw1, cv_w_pw2, xattn_wq, xattn_wo, mlp_w1, mlp_w2))
    n_mix, n_att, n_mlp = _rows3(norm_mix_w), _rows3(norm_xattn_w), _rows3(norm_mlp_w)
    k_all, v_all = _memory_kv(mem, mem_norm_w, xattn_wk, xattn_wv)
    for layer in range(depth):
        if layer % 2 == 0:
            e = layer // 2
            x = _mixer(x, n_mix, w_in_p, hgrn_lb_logits, hgrn_out_norm_w, ssd_conv_w, ssd_conv_b, dt_bias_p, a_log_p,
                       d_skip_x, ssd_norm_w, w_out, layer, e)
        else:
            x = _conformer(x, n_mix, w_pw1, cv_b_pw1, cv_w_dw, cv_b_dw, cv_ln_w, cv_ln_b, w_pw2, cv_b_pw2, layer,
                           layer // 2)
        x = _cross_attention(x, n_att, wq, k_all, v_all, wo, layer)
        x = _mlp(x.reshape(b * t, d), n_mlp, w1, w2, final_norm_w, layer,
                 final=(layer == depth - 1)).reshape(b, t, d)
    return x
```

```python
import functools

import jax
import jax.numpy as jnp
from jax import lax
from jax.experimental import pallas as pl
from jax.experimental.pallas import tpu as pltpu

F32 = jnp.float32
BF16 = jnp.bfloat16

EPS = 1e-6
CHUNK = 64
SUB = 16
LANES = 128
SUBLANES = 8
PAD_ROWS = 8

HGRN_HEADS = 4
HGRN_DK = 128
SSD_HEADS = 8
SSD_HEADDIM = 64
SSD_GROUPS = 2
SSD_STATE = 128
XATTN_HEADS = 4

VMEM_LIMIT_BYTES = 56 * 1024 * 1024


def _rms(x):
    return x * lax.rsqrt(jnp.mean(x * x, axis=-1, keepdims=True) + EPS)


def _sigmoid(x):
    return 1.0 / (1.0 + jnp.exp(-x))


def _silu(x):
    return x * _sigmoid(x)


def _dot(a, b):
    return jnp.dot(a, b, preferred_element_type=F32)


def _dot_nt(a, b):
    return lax.dot_general(a, b, (((1,), (1,)), ((), ())), preferred_element_type=F32)


def _dot_tn(a, b):
    return lax.dot_general(a, b, (((0,), (0,)), ((), ())), preferred_element_type=F32)


def _split3(a):
    a0 = a.astype(BF16)
    r1 = a - a0.astype(F32)
    a1 = r1.astype(BF16)
    a2 = (r1 - a1.astype(F32)).astype(BF16)
    return a0, a1, a2


def _dot01(m01, a):
    a0, a1, a2 = _split3(a)
    return _dot(m01, a0) + _dot(m01, a1) + _dot(m01, a2)


def _resident(shape):
    nd = len(shape)
    return pl.BlockSpec(shape, lambda *_: (0,) * nd, pipeline_mode=pl.Buffered(1))


def _layer(shape, layer):
    nd = len(shape)
    return pl.BlockSpec((None,) + tuple(shape), lambda *_: (layer,) + (0,) * nd, pipeline_mode=pl.Buffered(1))


def _rows3(a):
    return a.reshape(a.shape[0], 1, a.shape[-1])


def _params(n_axes):
    return pltpu.CompilerParams(dimension_semantics=("arbitrary",) * n_axes,
                                vmem_limit_bytes=VMEM_LIMIT_BYTES)


def _kv_body(mem_ref, mw_ref, wk_ref, wv_ref, k_ref, v_ref):
    mn = (_rms(mem_ref[...]) * mw_ref[...]).astype(BF16)
    k_ref[0] = _dot(mn, wk_ref[0].astype(BF16)).astype(BF16)
    v_ref[0] = _dot(mn, wv_ref[0].astype(BF16)).astype(BF16)


def _memory_kv(mem, mem_norm_w, wk, wv):
    depth, d, _ = wk.shape
    b, m, _ = mem.shape
    out = jax.ShapeDtypeStruct((depth, b * m, d), BF16)
    k_all, v_all = pl.pallas_call(
        _kv_body,
        grid=(depth,),
        in_specs=[
            _resident((b * m, d)),
            _resident((1, d)),
            pl.BlockSpec((1, d, d), lambda l: (l, 0, 0)),
            pl.BlockSpec((1, d, d), lambda l: (l, 0, 0)),
        ],
        out_specs=[
            pl.BlockSpec((1, b * m, d), lambda l: (l, 0, 0)),
            pl.BlockSpec((1, b * m, d), lambda l: (l, 0, 0)),
        ],
        out_shape=[out, out],
        compiler_params=_params(1),
        name="memory_kv",
    )(mem.reshape(b * m, d), mem_norm_w.reshape(1, d), wk, wv)
    return k_all.reshape(depth, b, m, d), v_all.reshape(depth, b, m, d)


def _mlp_body(x_ref, nw_ref, w1_ref, w2_ref, fw_ref, o_ref, *, ff_chunk, final):
    x = x_ref[...]
    hb = (_rms(x) * nw_ref[...]).astype(BF16)
    acc = x
    for c in range(w1_ref.shape[1] // ff_chunk):
        a = _dot(hb, w1_ref[:, c * ff_chunk:(c + 1) * ff_chunk])
        a = jnp.square(jnp.maximum(a, 0.0)).astype(BF16)
        acc = acc + _dot(a, w2_ref[c * ff_chunk:(c + 1) * ff_chunk, :])
    if final:
        acc = _rms(acc) * fw_ref[...]
    o_ref[...] = acc


def _mlp(x2, norm_w, w1, w2, final_w, layer, *, final, tb=1024, ff_chunk=1024):
    n, d = x2.shape
    dff = w1.shape[2]
    return pl.pallas_call(
        functools.partial(_mlp_body, ff_chunk=ff_chunk, final=final),
        grid=(n // tb,),
        in_specs=[
            pl.BlockSpec((tb, d), lambda i: (i, 0)),
            _layer((1, d), layer),
            _layer((d, dff), layer),
            _layer((dff, d), layer),
            _resident((1, d)),
        ],
        out_specs=pl.BlockSpec((tb, d), lambda i: (i, 0)),
        out_shape=jax.ShapeDtypeStruct((n, d), F32),
        compiler_params=_params(1),
        name="sq_relu_mlp",
    )(x2, norm_w, w1, w2, final_w.reshape(1, d))


def _attn_body(x_ref, nw_ref, wq_ref, k_ref, v_ref, wo_ref, o_ref, *, heads):
    x = x_ref[0]
    d = x.shape[-1]
    hd = d // heads
    hb = (_rms(x) * nw_ref[...]).astype(BF16)
    q = (_dot(hb, wq_ref[...]) * (hd ** -0.5)).astype(BF16)
    outs = []
    for h in range(heads):
        sl = slice(h * hd, (h + 1) * hd)
        s = _dot_nt(q[:, sl], k_ref[0, 0, :, sl])
        p = jnp.exp(s - jnp.max(s, axis=-1, keepdims=True))
        den = jnp.sum(p, axis=-1, keepdims=True)
        oh = _dot(p.astype(BF16), v_ref[0, 0, :, sl]) * (1.0 / den)
        outs.append(oh.astype(BF16))
    o = jnp.concatenate(outs, axis=-1)
    o_ref[0] = x + _dot(o, wo_ref[...])


def _cross_attention(x, norm_w, wq, k_all, v_all, wo, layer, *, tb=1024):
    b, t, d = x.shape
    m = k_all.shape[2]
    return pl.pallas_call(
        functools.partial(_attn_body, heads=XATTN_HEADS),
        grid=(b, t // tb),
        in_specs=[
            pl.BlockSpec((1, tb, d), lambda i, j: (i, j, 0)),
            _layer((1, d), layer),
            _layer((d, d), layer),
            pl.BlockSpec((1, 1, m, d), lambda i, j: (layer, i, 0, 0)),
            pl.BlockSpec((1, 1, m, d), lambda i, j: (layer, i, 0, 0)),
            _layer((d, d), layer),
        ],
        out_specs=pl.BlockSpec((1, tb, d), lambda i, j: (i, j, 0)),
        out_shape=jax.ShapeDtypeStruct((b, t, d), F32),
        compiler_params=_params(2),
        name="memory_xattn",
    )(x, norm_w, wq, k_all, v_all, wo)


def _slab_conv(u_ref, c_ref, w_taps, bias, *, hist, n_t, t_step, t_lo=None):
    taps = len(w_taps)
    n_slab = bias.shape[0]
    p_u = u_ref.shape[0] // n_slab
    p_c = c_ref.shape[0] // n_slab

    def step(i, carry):
        t0 = i * t_step if isinstance(i, int) else pl.multiple_of(i * t_step, t_step)
        accs = [bias] * t_step
        for k in range(t_step + taps - 1):
            uk = u_ref[pl.ds(t0 + hist - (taps - 1) + k, n_slab, stride=p_u), :]
            for tt in range(t_step):
                if 0 <= k - tt < taps:
                    accs[tt] = accs[tt] + w_taps[k - tt] * uk
        for tt in range(t_step):
            c_ref[pl.ds(t0 + tt, n_slab, stride=p_c), :] = accs[tt]
        return carry

    if t_lo is None:
        lax.fori_loop(0, n_t // t_step, step, 0)
    else:
        for i in range(t_lo // t_step, (t_lo + n_t) // t_step):
            step(i, 0)


def _slab_pitches(hist, n_t):
    return hist + n_t + 4, n_t + SUBLANES


def _conf_body(x_ref, nw_ref, w1_ref, b1_ref, wdw_ref, bdw_ref, lnw_ref, lnb_ref, w2_ref, b2_ref,
               o_ref, u_ref, c_ref, *, taps, hist, t_step):
    tb, d = x_ref.shape[1], x_ref.shape[2]
    n_slab = d // LANES
    p_u = u_ref.shape[0] // n_slab
    p_c = c_ref.shape[0] // n_slab

    @pl.when(pl.program_id(1) == 0)
    def _():
        for s in range(n_slab):
            u_ref[s * p_u:s * p_u + hist, :] = jnp.zeros((hist, LANES), F32)

    half = tb // 2
    w_taps = [wdw_ref[j] for j in range(taps)]

    def glu_to_slabs(ag, lo):
        u = ag[:, :d] * _sigmoid(ag[:, d:])
        for s in range(n_slab):
            u_ref[s * p_u + hist + lo:s * p_u + hist + lo + half, :] = u[:, s * LANES:(s + 1) * LANES]

    def finish(xh, lo):
        c = jnp.concatenate([c_ref[s * p_c + lo:s * p_c + lo + half, :] for s in range(n_slab)], axis=-1)
        mu = jnp.mean(c, axis=-1, keepdims=True)
        cc = c - mu
        var = jnp.mean(cc * cc, axis=-1, keepdims=True)
        yn = cc * lax.rsqrt(var + EPS) * lnw_ref[...] + lnb_ref[...]
        ys = _silu(yn).astype(BF16)
        o_ref[0, lo:lo + half, :] = xh + _dot(ys, w2_ref[...]) + b2_ref[...]

    x0 = x_ref[0, 0:half, :]
    x1 = x_ref[0, half:tb, :]
    ag0 = _dot((_rms(x0) * nw_ref[...]).astype(BF16), w1_ref[...]) + b1_ref[...]
    glu_to_slabs(ag0, 0)
    ag1 = _dot((_rms(x1) * nw_ref[...]).astype(BF16), w1_ref[...]) + b1_ref[...]
    _slab_conv(u_ref, c_ref, w_taps, bdw_ref[...], hist=hist, n_t=half, t_step=t_step, t_lo=0)
    glu_to_slabs(ag1, half)
    finish(x0, 0)
    _slab_conv(u_ref, c_ref, w_taps, bdw_ref[...], hist=hist, n_t=half, t_step=t_step, t_lo=half)
    for s in range(n_slab):
        u_ref[s * p_u:s * p_u + hist, :] = u_ref[s * p_u + tb:s * p_u + tb + hist, :]
    finish(x1, half)


def _conformer(x, norm_w, w_pw1, b_pw1, w_dw, b_dw, ln_w, ln_b, w_pw2, b_pw2, layer, o, *, tb=1024, t_step=16):
    b, t, d = x.shape
    taps = w_dw.shape[1]
    n_slab = d // LANES
    assert n_slab == SUBLANES
    hist = -(-(taps - 1) // SUBLANES) * SUBLANES
    p_u, p_c = _slab_pitches(hist, tb)
    return pl.pallas_call(
        functools.partial(_conf_body, taps=taps, hist=hist, t_step=t_step),
        grid=(b, t // tb),
        in_specs=[
            pl.BlockSpec((1, tb, d), lambda i, j: (i, j, 0)),
            _layer((1, d), layer),
            _layer((d, 2 * d), o),
            _layer((1, 2 * d), o),
            _layer((taps, n_slab, LANES), o),
            _layer((n_slab, LANES), o),
            _layer((1, d), o),
            _layer((1, d), o),
            _layer((d, d), o),
            _layer((1, d), o),
        ],
        out_specs=pl.BlockSpec((1, tb, d), lambda i, j: (i, j, 0)),
        out_shape=jax.ShapeDtypeStruct((b, t, d), F32),
        scratch_shapes=[pltpu.VMEM((n_slab * p_u, LANES), F32), pltpu.VMEM((n_slab * p_c, LANES), F32)],
        compiler_params=_params(2),
        name="conformer_conv",
    )(x, norm_w, w_pw1, _rows3(b_pw1), w_dw.reshape(-1, taps, n_slab, LANES), b_dw.reshape(-1, n_slab, LANES),
      _rows3(ln_w), _rows3(ln_b), w_pw2, _rows3(b_pw2))


def _iota2(shape, axis):
    return lax.broadcasted_iota(jnp.int32, shape, axis)


HGRN_GROUP = SUB * SUBLANES
BLOCK_PITCH = SUB + 4
GROUP_PITCH = SUBLANES * BLOCK_PITCH
LOG2E = 1.4426950408889634
LOCAL_UNROLL = 8


def _mix_body(x_ref, nw_ref, win_ref, lbl_ref, onw_ref, cw_ref, cb_ref, dtb_ref, alog_ref, dsk_ref,
              snw_ref, wout_ref, o_ref,
              pr_ref, qs_ref, ks_ref, bs_ref, vs_ref, od_ref, qd_ref, kd_ref, bd_ref, qx_ref, kx_ref,
              su_ref, sc_ref, xbc_ref, xc_ref, xe_ref, ecs_ref, cs_ref, csr_ref, yd_ref, oc_ref, sth_ref, sts_ref, y_ref, *, e, t_step):
    tb, d = x_ref.shape[1], x_ref.shape[2]
    hk = HGRN_HEADS * HGRN_DK
    sd = SSD_HEADS * SSD_HEADDIM
    gn = SSD_GROUPS * SSD_STATE
    c_q, c_f, c_i, c_g = 0, hk, 2 * hk, 3 * hk
    c_z = 4 * hk
    c_x = c_z + sd
    c_dt = c_x + sd + 2 * gn
    hp = SSD_HEADS // SSD_GROUPS * SSD_HEADDIM
    taps = cw_ref.shape[0]
    hist = PAD_ROWS
    n_slab = cb_ref.shape[0]
    p_u = su_ref.shape[0] // n_slab
    p_c = sc_ref.shape[0] // n_slab
    n_groups = tb // HGRN_GROUP
    n_chunks = tb // CHUNK

    @pl.when(pl.program_id(1) == 0)
    def _():
        for s in range(n_slab):
            su_ref[s * p_u:s * p_u + hist, :] = jnp.zeros((hist, LANES), F32)
        sth_ref[...] = jnp.zeros(sth_ref.shape, F32)
        sts_ref[...] = jnp.zeros(sts_ref.shape, F32)

    x = x_ref[0]
    hb = (_rms(x) * nw_ref[...]).astype(BF16)
    pr_ref[:, 0:c_g] = _dot(hb, win_ref[:, 0:c_g])

    lg = lbl_ref[...]
    ex = jnp.exp(lg - jnp.max(lg, axis=0, keepdims=True))
    probs = ex / jnp.sum(ex, axis=0, keepdims=True)
    lb = jnp.sum(probs[0:e + 1], axis=0, keepdims=True) - probs[0:1]

    rg = _iota2((HGRN_GROUP, HGRN_GROUP), 0)
    cg = _iota2((HGRN_GROUP, HGRN_GROUP), 1)
    same_chunk = (rg // CHUNK) == (cg // CHUNK)
    m01 = lambda cond: jnp.where(same_chunk & cond, 1.0, 0.0).astype(BF16)
    tri_incl = m01(cg <= rg)
    r64 = _iota2((CHUNK, CHUNK), 0)
    c64 = _iota2((CHUNK, CHUNK), 1)
    tri64 = jnp.where(c64 <= r64, 1.0, 0.0).astype(BF16)
    lane2 = _iota2((CHUNK, LANES), 1)
    causal2 = (lane2 % CHUNK) <= _iota2((CHUNK, LANES), 0)
    rb = _iota2((tb, tb), 0)
    cb_ = _iota2((tb, tb), 1)
    tri_rows = jnp.where(((rb // CHUNK) == (cb_ // CHUNK)) & (rb <= cb_), 1.0, 0.0).astype(BF16)
    n_pair = SSD_HEADS // 2
    er = _iota2((LANES, sd), 0)
    expand = jnp.where(_iota2((LANES, sd), 1) // SSD_HEADDIM == er, 1.0, 0.0).astype(BF16)
    ones_k = jnp.ones((HGRN_DK, HGRN_DK), BF16)
    pair_diag = (_iota2((LANES, LANES), 0) // SSD_HEADDIM) == (_iota2((LANES, LANES), 1) // SSD_HEADDIM)

    nb = CHUNK // SUB
    n_blk = HGRN_GROUP // SUB
    zero_row = jnp.zeros((1, HGRN_DK), F32)
    for g in range(n_groups):
        rows = slice(g * HGRN_GROUP, (g + 1) * HGRN_GROUP)
        for h in range(HGRN_HEADS):
            hs = slice(h * HGRN_DK, (h + 1) * HGRN_DK)
            lbh = lb[:, hs]
            fr = pr_ref[rows, c_f + h * HGRN_DK:c_f + (h + 1) * HGRN_DK]
            lf = jnp.log(lbh + (1.0 - lbh) * _sigmoid(fr)) * LOG2E
            kk = (1.0 - lbh) * _sigmoid(-fr)
            bcs = _dot01(tri_incl, lf)
            ends = [bcs[(i + 1) * SUB - 1:(i + 1) * SUB, :] for i in range(n_blk)]

            def at_block_end(back):
                rows_ = [ends[i - back] if (i % nb) - back >= 0 else zero_row for i in range(n_blk)]
                return jnp.concatenate([jnp.broadcast_to(r_, (SUB, HGRN_DK)) for r_ in rows_], axis=0)

            btot = jnp.concatenate(
                [jnp.broadcast_to(ends[(i // nb) * nb + nb - 1], (SUB, HGRN_DK)) for i in range(n_blk)], axis=0)
            qq = pr_ref[rows, c_q + h * HGRN_DK:c_q + (h + 1) * HGRN_DK] * (HGRN_DK ** -0.5)
            vv = pr_ref[rows, c_i + h * HGRN_DK:c_i + (h + 1) * HGRN_DK]
            qd_ref[rows, hs] = (qq * jnp.exp2(bcs)).astype(BF16)
            kd_ref[rows, hs] = (kk * jnp.exp2(btot - bcs)).astype(BF16)
            bd_ref[rows, hs] = jnp.exp2(btot)
            kx_ref[rows, hs] = (kk * jnp.exp2(at_block_end(0) - bcs)).astype(BF16)
            for m in range(1, nb):
                qx_ref[m - 1, rows, hs] = (qq * jnp.exp2(bcs - at_block_end(m))).astype(BF16)
            for blk in range(SUBLANES):
                src = slice(blk * SUB, (blk + 1) * SUB)
                lo = (h * n_groups + g) * GROUP_PITCH + blk * BLOCK_PITCH
                qs_ref[lo:lo + SUB, :] = qq[src, :]
                ks_ref[lo:lo + SUB, :] = kk[src, :]
                bs_ref[lo:lo + SUB, :] = bcs[src, :]
                vs_ref[lo:lo + SUB, :] = vv[src, :]

    pr_ref[:, c_g:] = _dot(hb, win_ref[:, c_g:])

    def pair_step(idx, carry):
        sb = pl.multiple_of(idx * GROUP_PITCH, SUBLANES)
        ob = pl.multiple_of(idx * HGRN_GROUP, HGRN_GROUP)
        ld = lambda ref, r: ref[pl.ds(sb + r, SUBLANES, stride=BLOCK_PITCH), :]
        qv = [ld(qs_ref, r) for r in range(SUB)]
        kv = [ld(ks_ref, r) for r in range(SUB)]
        bv = [ld(bs_ref, r) for r in range(SUB)]
        vv_ = [ld(vs_ref, r) for r in range(SUB)]
        prods = []
        for r in range(SUB):
            for s_ in range(r + 1):
                if s_ == r:
                    prods.append(qv[r] * kv[r])
                else:
                    prods.append(qv[r] * kv[s_] * jnp.exp2(bv[r] - bv[s_]))
        sc = _dot(jnp.concatenate(prods, axis=0).astype(BF16), ones_k)
        n = 0
        for r in range(SUB):
            acc = None
            for s_ in range(r + 1):
                term = sc[n * SUBLANES:(n + 1) * SUBLANES, :] * vv_[s_]
                acc = term if acc is None else acc + term
                n += 1
            od_ref[pl.ds(ob + r, SUBLANES, stride=SUB), :] = acc
        return carry

    lax.fori_loop(0, HGRN_HEADS * n_groups, pair_step, 0, unroll=8)

    for s in range(n_slab):
        su_ref[s * p_u + hist:s * p_u + hist + tb, :] = pr_ref[:, c_x + s * LANES:c_x + (s + 1) * LANES]
    _slab_conv(su_ref, sc_ref, [cw_ref[j] for j in range(taps)], cb_ref[...], hist=hist, n_t=tb, t_step=t_step)
    for s in range(n_slab):
        su_ref[s * p_u:s * p_u + hist, :] = su_ref[s * p_u + tb:s * p_u + tb + hist, :]
    xbc_ref[...] = _silu(jnp.concatenate([sc_ref[s * p_c:s * p_c + tb, :] for s in range(n_slab)], axis=-1))
    dt = jnp.logaddexp(pr_ref[:, c_dt:c_dt + LANES] + dtb_ref[...], 0.0)
    da = dt * (-jnp.exp(alog_ref[...]))
    d0, d1, d2 = _split3(da)
    for c in range(n_chunks):
        cr = slice(c * CHUNK, (c + 1) * CHUNK)
        cs_ref[cr, :] = _dot(tri64, d0[cr]) + _dot(tri64, d1[cr]) + _dot(tri64, d2[cr])
    cs = cs_ref[...]
    cs_end = jnp.concatenate(
        [jnp.broadcast_to(cs[(c + 1) * CHUNK - 1:(c + 1) * CHUNK, :], (CHUNK, LANES)) for c in range(n_chunks)], axis=0)
    cs_t = _dot_tn(d0, tri_rows) + _dot_tn(d1, tri_rows) + _dot_tn(d2, tri_rows)
    fill = [jnp.zeros((SUBLANES - n_pair, tb), F32)]
    cs_even = jnp.concatenate([cs_t[2 * p:2 * p + 1, :] for p in range(n_pair)] + fill, axis=0)
    cs_odd = jnp.concatenate([cs_t[2 * p + 1:2 * p + 2, :] for p in range(n_pair)] + fill, axis=0)
    for c in range(n_chunks):
        cl = slice(c * CHUNK, (c + 1) * CHUNK)
        csr_ref[c] = jnp.concatenate([cs_even[:, cl], cs_odd[:, cl]], axis=1)
    e0, e1, e2 = _split3(jnp.exp(cs))
    spread = _dot(jnp.concatenate([dt.astype(BF16), jnp.exp(cs_end - cs).astype(BF16), e0, e1, e2], axis=0), expand)
    xc = xbc_ref[:, 0:sd] * spread[0:tb]
    xc_ref[...] = xc.astype(BF16)
    xe_ref[...] = (xc * spread[tb:2 * tb]).astype(BF16)
    ecs_ref[...] = spread[2 * tb:3 * tb] + spread[3 * tb:4 * tb] + spread[4 * tb:5 * tb]

    def local_step(it, carry):
        nb = CHUNK // SUB
        lane64 = _iota2((CHUNK, CHUNK), 1)
        chunks = [it * LOCAL_UNROLL + u for u in range(LOCAL_UNROLL)]
        r0s = [pl.multiple_of(c * CHUNK, CHUNK) for c in chunks]
        stage1 = []
        for c, r0 in zip(chunks, r0s):
            rows = pl.ds(r0, CHUNK)
            res = []
            for h in range(HGRN_HEADS):
                hs = slice(h * HGRN_DK, (h + 1) * HGRN_DK)
                lhs = jnp.concatenate(
                    [qx_ref[i - j - 1, pl.ds(r0 + i * SUB, SUB), hs] for j in range(nb - 1) for i in range(j + 1, nb)],
                    axis=0)
                res.append(_dot_nt(lhs, kx_ref[rows, hs]))
            cbm2 = []
            for g in range(SSD_GROUPS):
                bm = xbc_ref[rows, sd + g * SSD_STATE:sd + (g + 1) * SSD_STATE].astype(BF16)
                cm = xbc_ref[rows, sd + gn + g * SSD_STATE:sd + gn + (g + 1) * SSD_STATE].astype(BF16)
                cbm2.append(_dot_nt(cm, jnp.concatenate([bm, bm], axis=0)))
            stage1.append((res, cbm2))
        for (c, r0), (res, cbm2) in zip(zip(chunks, r0s), stage1):
            rows = pl.ds(r0, CHUNK)
            for h in range(HGRN_HEADS):
                vi = pr_ref[rows, c_i + h * HGRN_DK:c_i + (h + 1) * HGRN_DK].astype(BF16)
                sc = jnp.zeros((CHUNK, CHUNK), F32)
                off = 0
                for j in range(nb - 1):
                    n_r = (nb - 1 - j) * SUB
                    col = jnp.concatenate([jnp.zeros((CHUNK - n_r, CHUNK), F32), res[h][off:off + n_r, :]], axis=0)
                    sc = jnp.where((lane64 // SUB) == j, col, sc)
                    off += n_r
                oc_ref[pl.ds(h * tb + r0, CHUNK), :] = _dot(sc.astype(BF16), vi)
            cs_c = cs_ref[rows, :]
            cs_rows = csr_ref[c]
            for p in range(n_pair):
                cs_pair = jnp.where(lane2 < CHUNK, cs_c[:, 2 * p:2 * p + 1], cs_c[:, 2 * p + 1:2 * p + 2])
                dec = jnp.where(causal2, jnp.exp(cs_pair - cs_rows[p:p + 1, :]), 0.0)
                xp = xc_ref[rows, p * LANES:(p + 1) * LANES]
                rhs = jnp.where(pair_diag, jnp.concatenate([xp, xp], axis=0), jnp.zeros((LANES, LANES), BF16))
                g = p // (n_pair // SSD_GROUPS)
                yd_ref[rows, p * LANES:(p + 1) * LANES] = _dot((cbm2[g] * dec).astype(BF16), rhs)
        return carry

    lax.fori_loop(0, n_chunks // LOCAL_UNROLL, local_step, 0)

    def state_step(c, carry):
        r0 = pl.multiple_of(c * CHUNK, CHUNK)
        rows = pl.ds(r0, CHUNK)
        gate = _silu(pr_ref[rows, c_g:c_g + hk])
        for h in range(HGRN_HEADS):
            hs = slice(h * HGRN_DK, (h + 1) * HGRN_DK)
            vi = pr_ref[rows, c_i + h * HGRN_DK:c_i + (h + 1) * HGRN_DK].astype(BF16)
            st = sth_ref[h]
            orow = pl.ds(h * tb + r0, CHUNK)
            o_h = od_ref[orow, :] + oc_ref[orow, :] + _dot_nt(qd_ref[rows, hs], st.astype(BF16))
            sth_ref[h] = st * bd_ref[pl.ds(r0, 1), hs] + _dot_tn(vi, kd_ref[rows, hs])
            o_h = _rms(o_h) * onw_ref[:, hs] * gate[:, hs]
            y_ref[rows, hs] = o_h.astype(BF16)

        xbc = xbc_ref[rows, :]
        ecs = ecs_ref[rows, :]
        xcd = xe_ref[rows, :]
        ys = []
        for g in range(SSD_GROUPS):
            bm = xbc[:, sd + g * SSD_STATE:sd + (g + 1) * SSD_STATE].astype(BF16)
            cm = xbc[:, sd + gn + g * SSD_STATE:sd + gn + (g + 1) * SSD_STATE].astype(BF16)
            gl = slice(g * hp, (g + 1) * hp)
            st = sts_ref[g]
            ys.append(_dot(cm, st.astype(BF16)) * ecs[:, gl])
            sts_ref[g] = st * ecs[CHUNK - 1:CHUNK, gl] + _dot_tn(bm, xcd[:, gl])
        y = jnp.concatenate(ys, axis=-1) + yd_ref[rows, :] + xbc[:, 0:sd] * dsk_ref[...]
        y = y * _silu(pr_ref[rows, c_z:c_z + sd])
        y = jnp.concatenate([_rms(y[:, g * hp:(g + 1) * hp]) for g in range(SSD_GROUPS)], axis=-1) * snw_ref[...]
        y_ref[rows, hk:hk + sd] = y.astype(BF16)
        return carry

    lax.fori_loop(0, n_chunks, state_step, 0, unroll=n_chunks)

    o_ref[0] = x + _dot(y_ref[...], wout_ref[...])


def _mixer_weights(w_in, dt_bias, a_log, d_skip):
    ab_in = w_in.shape[2]
    pw = -(-ab_in // LANES) * LANES
    w_in_p = jnp.pad(w_in, ((0, 0), (0, 0), (0, pw - ab_in))).astype(BF16)
    pad8 = lambda a: _rows3(jnp.pad(a, ((0, 0), (0, LANES - a.shape[1]))))
    return w_in_p, pad8(dt_bias), pad8(a_log), _rows3(jnp.repeat(d_skip, SSD_HEADDIM, axis=1))


def _mixer(x, norm_w, w_in_p, lb_logits, out_norm_w, conv_w, conv_b, dt_bias_p, a_log_p, d_skip_x, ssd_norm_w,
           w_out, layer, e, *, tb=512, t_step=16):
    b, t, d = x.shape
    n_even, hk = lb_logits.shape
    sd = SSD_HEADS * SSD_HEADDIM
    pw = w_in_p.shape[2]
    taps, cc = conv_w.shape[1:]
    n_slab = cc // LANES
    assert tb % HGRN_GROUP == 0 and n_slab == SUBLANES and taps - 1 <= PAD_ROWS
    p_u, p_c = _slab_pitches(PAD_ROWS, tb)
    slab_rows = HGRN_HEADS * (tb // HGRN_GROUP) * GROUP_PITCH
    slab = pltpu.VMEM((slab_rows, LANES), F32)
    return pl.pallas_call(
        functools.partial(_mix_body, e=e, t_step=t_step),
        grid=(b, t // tb),
        in_specs=[
            pl.BlockSpec((1, tb, d), lambda i, j: (i, j, 0)),
            _layer((1, d), layer),
            _layer((d, pw), e),
            _resident((n_even, hk)),
            _layer((1, hk), e),
            _layer((taps, n_slab, LANES), e),
            _layer((n_slab, LANES), e),
            _layer((1, LANES), e),
            _layer((1, LANES), e),
            _layer((1, sd), e),
            _layer((1, sd), e),
            _layer((hk + sd, d), e),
        ],
        out_specs=pl.BlockSpec((1, tb, d), lambda i, j: (i, j, 0)),
        out_shape=jax.ShapeDtypeStruct((b, t, d), F32),
        scratch_shapes=[
            pltpu.VMEM((tb, pw), F32),
            slab, slab, slab, slab,
            pltpu.VMEM((HGRN_HEADS * tb, LANES), F32),
            pltpu.VMEM((tb, hk), BF16),
            pltpu.VMEM((tb, hk), BF16),
            pltpu.VMEM((tb, hk), F32),
            pltpu.VMEM((CHUNK // SUB - 1, tb, hk), BF16),
            pltpu.VMEM((tb, hk), BF16),
            pltpu.VMEM((n_slab * p_u, LANES), F32),
            pltpu.VMEM((n_slab * p_c, LANES), F32),
            pltpu.VMEM((tb, cc), F32),
            pltpu.VMEM((tb, sd), BF16),
            pltpu.VMEM((tb, sd), BF16),
            pltpu.VMEM((tb, sd), F32),
            pltpu.VMEM((tb, LANES), F32),
            pltpu.VMEM((tb // CHUNK, SUBLANES, LANES), F32),
            pltpu.VMEM((tb, sd), F32),
            pltpu.VMEM((HGRN_HEADS * tb, LANES), F32),
            pltpu.VMEM((HGRN_HEADS, HGRN_DK, HGRN_DK), F32),
            pltpu.VMEM((SSD_GROUPS, SSD_STATE, sd // SSD_GROUPS), F32),
            pltpu.VMEM((tb, hk + sd), BF16),
        ],
        compiler_params=_params(2),
        name="hgrn_ssd_mixer",
    )(x, norm_w, w_in_p, lb_logits, _rows3(out_norm_w), conv_w.reshape(-1, taps, n_slab, LANES),
      conv_b.reshape(-1, n_slab, LANES), dt_bias_p, a_log_p, d_skip_x, _rows3(ssd_norm_w), w_out)


def kernel(x, mem, mem_norm_w, norm_mix_w, ab_w_in, hgrn_lb_logits, hgrn_out_norm_w, ssd_conv_w, ssd_conv_b,
           ssd_dt_bias, ssd_a_log, ssd_d, ssd_norm_w, ab_w_out, cv_w_pw1, cv_b_pw1, cv_w_dw, cv_b_dw, cv_ln_w,
           cv_ln_b, cv_w_pw2, cv_b_pw2, norm_xattn_w, xattn_wq, xattn_wk, xattn_wv, xattn_wo, norm_mlp_w,
           mlp_w1, mlp_w2, final_norm_w):
    b, t, d = x.shape
    depth = norm_mix_w.shape[0]
    bf = lambda a: a.astype(BF16)
    w_in_p, dt_bias_p, a_log_p, d_skip_x = _mixer_weights(ab_w_in, ssd_dt_bias, ssd_a_log, ssd_d)
    w_out, w_pw1, w_pw2, wq, wo, w1, w2 = map(bf, (ab_w_out, cv_w_pw1, cv_w_pw2, xattn_wq, xattn_wo, mlp_w1, mlp_w2))
    n_mix, n_att, n_mlp = _rows3(norm_mix_w), _rows3(norm_xattn_w), _rows3(norm_mlp_w)
    k_all, v_all = _memory_kv(mem, mem_norm_w, xattn_wk, xattn_wv)
    for layer in range(depth):
        if layer % 2 == 0:
            e = layer // 2
            x = _mixer(x, n_mix, w_in_p, hgrn_lb_logits, hgrn_out_norm_w, ssd_conv_w, ssd_conv_b, dt_bias_p, a_log_p,
                       d_skip_x, ssd_norm_w, w_out, layer, e)
        else:
            x = _conformer(x, n_mix, w_pw1, cv_b_pw1, cv_w_dw, cv_b_dw, cv_ln_w, cv_ln_b, w_pw2, cv_b_pw2, layer,
                           layer // 2)
        x = _cross_attention(x, n_att, wq, k_all, v_all, wo, layer)
        x = _mlp(x.reshape(b * t, d), n_mlp, w1, w2, final_norm_w, layer,
                 final=(layer == depth - 1)).reshape(b, t, d)
    return x
```
